```python
import math
import jax, jax.numpy as jnp
from jax import lax
import numpy as np

D_MODEL = 1024
BATCH = 2
SEQ = 8192
DEPTH = 2

MEM_LEN = 256
N_MIXERS = 2
N_A = (DEPTH + N_MIXERS - 1) // N_MIXERS
N_B = DEPTH // N_MIXERS
ROPE_THETA = 10000.0
NORM_EPS = 1e-6
DA_HEADS = D_MODEL // 128
DA_QK_DIM = 64
DA_V_DIM = 2 * DA_QK_DIM
DA_Q_BLOCK = 128
DB_PAIRS = ((128, 1), (512, 4), (2048, 16))
DB_SLOTS = 8
DB_HEAD_DIM = 128
DB_Q_BLOCK = 64
XA_HEADS = 4
XA_HEAD_DIM = D_MODEL // XA_HEADS
D_FF = 4 * D_MODEL

kernel_name = 'hybrid_diff_dilated_encoder'


def rms_norm(x, g):
    xf = x.astype(jnp.float32)
    y = xf * lax.rsqrt(jnp.mean(xf * xf, axis=-1, keepdims=True) + NORM_EPS)
    return (y * g.astype(jnp.float32)).astype(x.dtype)


def rope(t, positions):
    dh = t.shape[-1]
    inv = ROPE_THETA ** (-jnp.arange(0, dh, 2, dtype=jnp.float32) / dh)
    ang = positions.astype(jnp.float32)[..., None] * inv
    ang = ang.reshape(ang.shape[:2] + (1,) * (t.ndim - 3) + (dh // 2,))
    cos, sin = jnp.cos(ang), jnp.sin(ang)
    tf = t.astype(jnp.float32)
    t1, t2 = tf[..., : dh // 2], tf[..., dh // 2:]
    return jnp.concatenate([t1 * cos - t2 * sin, t1 * sin + t2 * cos], axis=-1).astype(t.dtype)


def diff_attention(h, positions, w_qkv, lam, subln_g, w_o, lambda_init):
    B, S, _ = h.shape
    H, dq, dv = DA_HEADS, DA_QK_DIM, DA_V_DIM
    qkv = h @ w_qkv
    q, k, v = jnp.split(qkv, [H * 2 * dq, 2 * H * 2 * dq], axis=-1)
    q = rope(q.reshape(B, S, H, 2, dq), positions)
    k = rope(k.reshape(B, S, H, 2, dq), positions)
    v = v.reshape(B, S, H, dv)
    lamf = lam.astype(jnp.float32)
    lam_full = jnp.exp(jnp.sum(lamf[0] * lamf[1])) - jnp.exp(jnp.sum(lamf[2] * lamf[3])) + lambda_init
    nblk = S // DA_Q_BLOCK
    qb = q.reshape(B, nblk, DA_Q_BLOCK, H, 2, dq).transpose(1, 0, 2, 3, 4, 5)
    scale = dq ** -0.5

    def block(qi):
        s = jnp.einsum('bqhcd,bkhcd->bhcqk', qi, k).astype(jnp.float32) * scale
        p = jax.nn.softmax(s, axis=-1)
        a = (p[:, :, 0] - lam_full * p[:, :, 1]).astype(v.dtype)
        return jnp.einsum('bhqk,bkhd->bqhd', a, v)

    o = lax.map(block, qb)
    o = o.transpose(1, 0, 2, 3, 4).reshape(B, S, H, dv)
    o = rms_norm(o, subln_g) * (1.0 - lambda_init)
    return o.reshape(B, S, H * dv) @ w_o


def banded_attention(q, k, v, radius):
    N, L, H, Dh = q.shape
    QB = DB_Q_BLOCK
    nb = -(-L // QB)
    Lp = nb * QB
    KB = QB + 2 * radius
    q = jnp.pad(q, ((0, 0), (0, Lp - L), (0, 0), (0, 0)))
    kpad = ((0, 0), (radius, Lp - L + radius), (0, 0), (0, 0))
    k = jnp.pad(k, kpad)
    v = jnp.pad(v, kpad)
    idx = (jnp.arange(nb) * QB)[:, None] + jnp.arange(KB)[None, :]
    kb = k[:, idx]
    vb = v[:, idx]
    qb = q.reshape(N, nb, QB, H, Dh)
    rel = jnp.arange(KB)[None, :] - radius - jnp.arange(QB)[:, None]
    key_pos = idx - radius
    mask = (jnp.abs(rel) <= radius)[None] & ((key_pos >= 0) & (key_pos < L))[:, None, :]
    s = jnp.einsum('nbqhd,nbkhd->nbhqk', qb, kb).astype(jnp.float32) * (Dh ** -0.5)
    s = jnp.where(mask[None, :, None], s, -jnp.inf)
    m = jnp.max(s, axis=-1, keepdims=True)
    p = jnp.exp(s - m)
    den = jnp.sum(p, axis=-1, keepdims=True)
    o = jnp.einsum('nbhqk,nbkhd->nbqhd', (p / den).astype(v.dtype), vb)
    lse = (m + jnp.log(den))[..., 0]
    o = o.reshape(N, Lp, H, Dh)[:, :L]
    lse = lse.transpose(0, 1, 3, 2).reshape(N, Lp, H)[:, :L]
    return o, lse


def dilated_group(q, k, v, dilation, radius):
    B, S, H, Dh = q.shape
    L = S // dilation

    def split(t):
        return t.reshape(B, L, dilation, H, Dh).transpose(0, 2, 1, 3, 4).reshape(B * dilation, L, H, Dh)

    o, lse = banded_attention(split(q), split(k), split(v), radius)
    o = o.reshape(B, dilation, L, H, Dh).transpose(0, 2, 1, 3, 4).reshape(B, S, H, Dh)
    lse = lse.reshape(B, dilation, L, H).transpose(0, 2, 1, 3).reshape(B, S, H)
    return o, lse


def dilated_attention(h, positions, w_qkv, w_o):
    B, S, _ = h.shape
    G, H, Dh = len(DB_PAIRS), DB_SLOTS, DB_HEAD_DIM
    qkv = (h @ w_qkv).reshape(B, S, 3, G, H, Dh)
    q = rope(qkv[:, :, 0], positions)
    k = rope(qkv[:, :, 1], positions)
    v = qkv[:, :, 2]
    outs, lses = [], []
    for g, (window, dil) in enumerate(DB_PAIRS):
        o, lse = dilated_group(q[:, :, g], k[:, :, g], v[:, :, g], dil, (window // 2) // dil)
        outs.append(o)
        lses.append(lse)
    wts = jax.nn.softmax(jnp.stack(lses), axis=0)
    o = jnp.einsum('gbsh,gbshd->bshd', wts, jnp.stack(outs).astype(jnp.float32)).astype(h.dtype)
    return o.reshape(B, S, H * Dh) @ w_o


def memory_cross_attention(h, mem, mem_g, wq, wkv, wo):
    B, S, _ = h.shape
    H, Dh = XA_HEADS, XA_HEAD_DIM
    M = mem.shape[1]
    mn = rms_norm(mem, mem_g)
    q = (h @ wq).reshape(B, S, H, Dh)
    kv = (mn @ wkv).reshape(B, M, 2, H, Dh)
    s = jnp.einsum('bqhd,bkhd->bhqk', q, kv[:, :, 0]).astype(jnp.float32) * (Dh ** -0.5)
    p = jax.nn.softmax(s, axis=-1).astype(h.dtype)
    o = jnp.einsum('bhqk,bkhd->bqhd', p, kv[:, :, 1])
    return o.reshape(B, S, H * Dh) @ wo


def squared_relu_mlp(h, w_up, w_down):
    return jnp.square(jax.nn.relu(h @ w_up)) @ w_down


def setup_inputs(seed: int = 0) -> dict:
    key = jax.random.key(seed)
    ks = jax.random.split(key, 24)

    def gain(k, shape):
        return 1.0 + 0.05 * jax.random.normal(k, shape, jnp.float32)

    def weight(k, shape, fan_in):
        return jax.random.normal(k, shape, jnp.float32) * (fan_in ** -0.5)

    D = D_MODEL
    da_w = DA_HEADS * DA_V_DIM
    db_w = len(DB_PAIRS) * DB_SLOTS * DB_HEAD_DIM
    xa_w = XA_HEADS * XA_HEAD_DIM
    x = jax.random.normal(ks[0], (BATCH, SEQ, D), jnp.float32)
    mem = jax.random.normal(ks[1], (BATCH, MEM_LEN, D), jnp.float32)
    positions = (jnp.arange(SEQ, dtype=jnp.int32)[None, :]
                 + jax.random.randint(ks[2], (BATCH, 1), 0, 1024, dtype=jnp.int32))
    return {
        'x': x,
        'mem': mem,
        'positions': positions,
        'mix_pre_g': gain(ks[3], (DEPTH, D)),
        'mix_post_g': gain(ks[4], (DEPTH, D)),
        'da_w_qkv': weight(ks[5], (N_A, D, 3 * da_w), D),
        'da_lambda': 0.1 * jax.random.normal(ks[6], (N_A, 4, DA_QK_DIM), jnp.float32),
        'da_subln_g': gain(ks[7], (N_A, DA_V_DIM)),
        'da_w_o': weight(ks[8], (N_A, da_w, D), da_w),
        'db_w_qkv': weight(ks[9], (N_B, D, 3 * db_w), D),
        'db_w_o': weight(ks[10], (N_B, DB_SLOTS * DB_HEAD_DIM, D), DB_SLOTS * DB_HEAD_DIM),
        'xa_pre_g': gain(ks[11], (DEPTH, D)),
        'mem_g': gain(ks[12], (DEPTH, D)),
        'xa_wq': weight(ks[13], (DEPTH, D, xa_w), D),
        'xa_wkv': weight(ks[14], (DEPTH, D, 2 * xa_w), D),
        'xa_wo': weight(ks[15], (DEPTH, xa_w, D), xa_w),
        'xa_post_g': gain(ks[16], (DEPTH, D)),
        'ffn_pre_g': gain(ks[17], (DEPTH, D)),
        'w_up': weight(ks[18], (DEPTH, D, D_FF), D),
        'w_down': weight(ks[19], (DEPTH, D_FF, D), D_FF),
        'ffn_post_g': gain(ks[20], (DEPTH, D)),
    }


def reference(x, mem, positions, mix_pre_g, mix_post_g, da_w_qkv, da_lambda, da_subln_g, da_w_o,
              db_w_qkv, db_w_o, xa_pre_g, mem_g, xa_wq, xa_wkv, xa_wo, xa_post_g,
              ffn_pre_g, w_up, w_down, ffn_post_g):
    for i in range(DEPTH):
        j = i // N_MIXERS
        h = rms_norm(x, mix_pre_g[i])
        if i % N_MIXERS == 0:
            lambda_init = 0.8 - 0.6 * math.exp(-0.3 * i)
            mix = diff_attention(h, positions, da_w_qkv[j], da_lambda[j], da_subln_g[j], da_w_o[j], lambda_init)
        else:
            mix = dilated_attention(h, positions, db_w_qkv[j], db_w_o[j])
        x = x + rms_norm(mix, mix_post_g[i])
        h = rms_norm(x, xa_pre_g[i])
        x = x + rms_norm(memory_cross_attention(h, mem, mem_g[i], xa_wq[i], xa_wkv[i], xa_wo[i]), xa_post_g[i])
        h = rms_norm(x, ffn_pre_g[i])
        x = x + rms_norm(squared_relu_mlp(h, w_up[i], w_down[i]), ffn_post_g[i])
    return x
```

```python
import functools
import math

import jax
import jax.numpy as jnp
from jax import lax
from jax.experimental import pallas as pl
from jax.experimental.pallas import tpu as pltpu

F32 = jnp.float32
BF16 = jnp.bfloat16

NORM_EPS = 1e-6
ROPE_THETA = 10000.0
LANES = 128
HEAD_W = 128
DA_HEADS = 8
DA_QK_DIM = 64
DB_GROUPS = ((128, 1), (512, 4), (2048, 16))
DB_HEADS = 8
DB_RADIUS = 64
XA_HEADS = 4
XA_HEAD_DIM = 256
NEG_BIG = -1e30
VMEM_LIMIT = 48 * 1024 * 1024

ROW_TILE = 512
ATT_TQ = 512
ATT_TK = 512
BAND_LT = 512
BAND_QB = 128


def _params(*sem):
    return pltpu.CompilerParams(dimension_semantics=sem, vmem_limit_bytes=VMEM_LIMIT)


def _dot(a, b):
    return jnp.dot(a, b, preferred_element_type=F32)


def _dot_nt(a, b):
    return lax.dot_general(a, b, (((1,), (1,)), ((), ())), preferred_element_type=F32)


def _rms(x, g):
    ms = jnp.mean(x * x, axis=-1, keepdims=True)
    return x * lax.rsqrt(ms + NORM_EPS) * g


def _rope_slab(t, cos, sin, half):
    if half == 64:
        rot = pltpu.roll(t, 64, 1)
    else:
        lane = lax.broadcasted_iota(jnp.int32, t.shape, 1)
        rot = jnp.where((lane & 32) == 0, pltpu.roll(t, 96, 1), pltpu.roll(t, 32, 1))
    return t * cos + rot * sin


def _rope_table_kernel(pos_ref, inv_ref, sign_ref, cos_ref, sin_ref):
    ang = pos_ref[...] * inv_ref[...]
    cos_ref[...] = jnp.cos(ang)
    sin_ref[...] = jnp.sin(ang) * sign_ref[...]


def _rope_tables(pos_col, dh):
    t = pos_col.shape[0]
    inv = ROPE_THETA ** (-jnp.arange(0, dh, 2, dtype=F32) / dh)
    reps = LANES // dh
    inv_full = jnp.tile(jnp.concatenate([inv, inv]), reps).reshape(1, LANES)
    sign = jnp.tile(jnp.concatenate([-jnp.ones(dh // 2, F32), jnp.ones(dh // 2, F32)]),
                    reps).reshape(1, LANES)
    tm = min(t, 2048)
    return pl.pallas_call(
        _rope_table_kernel,
        grid=(t // tm,),
        in_specs=[pl.BlockSpec((tm, 1), lambda i: (i, 0)),
                  pl.BlockSpec((1, LANES), lambda i: (0, 0)),
                  pl.BlockSpec((1, LANES), lambda i: (0, 0))],
        out_specs=[pl.BlockSpec((tm, LANES), lambda i: (i, 0)),
                   pl.BlockSpec((tm, LANES), lambda i: (i, 0))],
        out_shape=[jax.ShapeDtypeStruct((t, LANES), F32)] * 2,
        compiler_params=_params("parallel"),
        name="rope_tables",
    )(pos_col, inv_full, sign)


def _qkv_diff_kernel(x_ref, g_ref, w_ref, cos_ref, sin_ref, qt_ref, k_ref, vt_ref, *, scale):
    d = x_ref.shape[1]
    h = _rms(x_ref[...], g_ref[...]).astype(BF16)
    cos = cos_ref[...]
    sin = sin_ref[...]
    q = _dot(h, w_ref[:, 0:d])
    for hh in range(DA_HEADS):
        sl = slice(hh * HEAD_W, (hh + 1) * HEAD_W)
        t = _rope_slab(q[:, sl], cos, sin, DA_QK_DIM // 2) * scale
        qt_ref[0, 0, sl, :] = t.T.astype(BF16)
    k = _dot(h, w_ref[:, d:2 * d])
    for hh in range(DA_HEADS):
        sl = slice(hh * HEAD_W, (hh + 1) * HEAD_W)
        k_ref[:, sl] = _rope_slab(k[:, sl], cos, sin, DA_QK_DIM // 2).astype(BF16)
    v = _dot(h, w_ref[:, 2 * d:3 * d])
    for hh in range(DA_HEADS):
        sl = slice(hh * HEAD_W, (hh + 1) * HEAD_W)
        vt_ref[0, 0, sl, :] = v[:, sl].T.astype(BF16)


def _qkv_diff(x2, g, w_bf, cos, sin, batch, seq):
    t, d = x2.shape
    tm = min(ATT_TQ, seq)
    nst = seq // tm
    scale = DA_QK_DIM ** -0.5
    tile_spec = pl.BlockSpec((1, 1, d, tm), lambda i: (i // nst, i % nst, 0, 0))
    return pl.pallas_call(
        functools.partial(_qkv_diff_kernel, scale=scale),
        grid=(t // tm,),
        in_specs=[pl.BlockSpec((tm, d), lambda i: (i, 0)),
                  pl.BlockSpec((1, d), lambda i: (0, 0)),
                  pl.BlockSpec((d, 3 * d), lambda i: (0, 0)),
                  pl.BlockSpec((tm, LANES), lambda i: (i, 0)),
                  pl.BlockSpec((tm, LANES), lambda i: (i, 0))],
        out_specs=[tile_spec, pl.BlockSpec((tm, d), lambda i: (i, 0)), tile_spec],
        out_shape=[jax.ShapeDtypeStruct((batch, nst, d, tm), BF16),
                   jax.ShapeDtypeStruct((t, d), BF16),
                   jax.ShapeDtypeStruct((batch, nst, d, tm), BF16)],
        compiler_params=_params("parallel"),
        name="qkv_diff",
    )(x2, g, w_bf, cos, sin)


def _diff_attn_kernel(qt_ref, k_ref, vt_ref, lam_ref, g_ref, o_ref,
                      qp_ref, m_ref, l_ref, acc_ref, *, nk, tk, lambda_init):
    qt = qt_ref[0, 0]
    row = lax.broadcasted_iota(jnp.int32, qt.shape, 0)
    zero = jnp.zeros_like(qt)
    qp_ref[0] = jnp.where(row < DA_QK_DIM, qt, zero)
    qp_ref[1] = jnp.where(row >= DA_QK_DIM, qt, zero)
    m_ref[...] = jnp.full(m_ref.shape, NEG_BIG, F32)
    l_ref[...] = jnp.zeros(l_ref.shape, F32)
    acc_ref[...] = jnp.zeros(acc_ref.shape, F32)

    def kv_step(ki, carry):
        kb = k_ref[0, pl.ds(pl.multiple_of(ki * tk, tk), tk), :]
        vt = vt_ref[0, ki]
        for c in range(2):
            s = _dot(kb, qp_ref[c])
            m_prev = m_ref[c]
            m_new = jnp.maximum(m_prev, jnp.max(s, axis=0, keepdims=True))
            alpha = jnp.exp(m_prev - m_new)
            e = jnp.exp(s - m_new)
            l_ref[c] = alpha * l_ref[c] + jnp.sum(e, axis=0, keepdims=True)
            acc_ref[c] = alpha * acc_ref[c] + _dot(vt, e.astype(BF16))
            m_ref[c] = m_new
        return carry

    lax.fori_loop(0, nk, kv_step, 0)

    lam = lam_ref[...]
    lam_full = (jnp.exp(jnp.sum(lam[0:1] * lam[1:2], axis=-1, keepdims=True))
                - jnp.exp(jnp.sum(lam[2:3] * lam[3:4], axis=-1, keepdims=True))
                + lambda_init)
    ot = acc_ref[0] * (1.0 / l_ref[0]) - lam_full * (acc_ref[1] * (1.0 / l_ref[1]))
    o = _rms(ot.T, g_ref[...]) * (1.0 - lambda_init)
    o_ref[0] = o.astype(BF16)


def _diff_attention(qt, k3, vt, lam, subln_g, lambda_init):
    batch, nq, d, tq = qt.shape
    seq = k3.shape[1]
    tk = vt.shape[3]
    nk = vt.shape[1]
    return pl.pallas_call(
        functools.partial(_diff_attn_kernel, nk=nk, tk=tk, lambda_init=lambda_init),
        grid=(batch, DA_HEADS, nq),
        in_specs=[pl.BlockSpec((1, 1, HEAD_W, tq), lambda b, h, i: (b, i, h, 0)),
                  pl.BlockSpec((1, seq, HEAD_W), lambda b, h, i: (b, 0, h)),
                  pl.BlockSpec((1, nk, HEAD_W, tk), lambda b, h, i: (b, 0, h, 0)),
                  pl.BlockSpec(lam.shape, lambda b, h, i: (0, 0)),
                  pl.BlockSpec((1, HEAD_W), lambda b, h, i: (0, 0))],
        out_specs=pl.BlockSpec((1, tq, HEAD_W), lambda b, h, i: (b, i, h)),
        out_shape=jax.ShapeDtypeStruct((batch, seq, d), BF16),
        scratch_shapes=[pltpu.VMEM((2, HEAD_W, tq), BF16),
                        pltpu.VMEM((2, 1, tq), F32),
                        pltpu.VMEM((2, 1, tq), F32),
                        pltpu.VMEM((2, HEAD_W, tq), F32)],
        compiler_params=_params("parallel", "parallel", "parallel"),
        name="diff_attention",
    )(qt, k3, vt, lam, subln_g)


def _proj_residual_kernel(a_ref, w_ref, g_ref, x_ref, o_ref):
    y = _dot(a_ref[...], w_ref[...])
    o_ref[...] = x_ref[...] + _rms(y, g_ref[...])


def _proj_residual(a2, w_bf, g, x2):
    t, d = x2.shape
    tm = min(ROW_TILE, t)
    return pl.pallas_call(
        _proj_residual_kernel,
        grid=(t // tm,),
        in_specs=[pl.BlockSpec((tm, a2.shape[1]), lambda i: (i, 0)),
                  pl.BlockSpec(w_bf.shape, lambda i: (0, 0)),
                  pl.BlockSpec((1, d), lambda i: (0, 0)),
                  pl.BlockSpec((tm, d), lambda i: (i, 0))],
        out_specs=pl.BlockSpec((tm, d), lambda i: (i, 0)),
        out_shape=jax.ShapeDtypeStruct((t, d), F32),
        compiler_params=_params("parallel"),
        name="proj_residual",
    )(a2, w_bf, g, x2)


def _mem_kv_kernel(mem_ref, g_ref, w_ref, o_ref):
    mn = _rms(mem_ref[...], g_ref[...]).astype(BF16)
    o_ref[...] = _dot(mn, w_ref[...]).astype(BF16)


def _mem_kv(mem2, g, w_bf):
    rows, d = mem2.shape
    n = w_bf.shape[1]
    tn = n // 2
    return pl.pallas_call(
        _mem_kv_kernel,
        grid=(n // tn,),
        in_specs=[pl.BlockSpec((rows, d), lambda j: (0, 0)),
                  pl.BlockSpec((1, d), lambda j: (0, 0)),
                  pl.BlockSpec((d, tn), lambda j: (0, j))],
        out_specs=pl.BlockSpec((rows, tn), lambda j: (0, j)),
        out_shape=jax.ShapeDtypeStruct((rows, n), BF16),
        compiler_params=_params("parallel"),
        name="mem_kv",
    )(mem2, g, w_bf)


def _cross_attn_kernel(x_ref, pre_g_ref, wq_ref, k_ref, v_ref, wo_ref, post_g_ref, o_ref):
    x = x_ref[...]
    h = _rms(x, pre_g_ref[...]).astype(BF16)
    q = (_dot(h, wq_ref[...]) * (XA_HEAD_DIM ** -0.5)).astype(BF16)
    outs = []
    for hh in range(XA_HEADS):
        sl = slice(hh * XA_HEAD_DIM, (hh + 1) * XA_HEAD_DIM)
        s = _dot_nt(q[:, sl], k_ref[:, sl])
        e = jnp.exp(s - jnp.max(s, axis=-1, keepdims=True))
        p = e * (1.0 / jnp.sum(e, axis=-1, keepdims=True))
        outs.append(_dot(p.astype(BF16), v_ref[:, sl]).astype(BF16))
    o = jnp.concatenate(outs, axis=1)
    y = _dot(o, wo_ref[...])
    o_ref[...] = x + _rms(y, post_g_ref[...])


def _cross_attention(x2, pre_g, wq_bf, kv, wo_bf, post_g, seq, mem_len):
    t, d = x2.shape
    tm = min(ROW_TILE, seq)
    nst = seq // tm
    return pl.pallas_call(
        _cross_attn_kernel,
        grid=(t // tm,),
        in_specs=[pl.BlockSpec((tm, d), lambda i: (i, 0)),
                  pl.BlockSpec((1, d), lambda i: (0, 0)),
                  pl.BlockSpec((d, d), lambda i: (0, 0)),
                  pl.BlockSpec((mem_len, d), lambda i: (i // nst, 0)),
                  pl.BlockSpec((mem_len, d), lambda i: (i // nst, 1)),
                  pl.BlockSpec((d, d), lambda i: (0, 0)),
                  pl.BlockSpec((1, d), lambda i: (0, 0))],
        out_specs=pl.BlockSpec((tm, d), lambda i: (i, 0)),
        out_shape=jax.ShapeDtypeStruct((t, d), F32),
        compiler_params=_params("parallel"),
        name="cross_attention",
    )(x2, pre_g, wq_bf, kv, kv, wo_bf, post_g)


def _mlp_kernel(x_ref, pre_g_ref, wu_ref, wd_ref, post_g_ref, o_ref, *, chunk):
    x = x_ref[...]
    h = _rms(x, pre_g_ref[...]).astype(BF16)
    d_ff = wu_ref.shape[1]
    acc = jnp.zeros(x.shape, F32)
    for c in range(d_ff // chunk):
        sl = slice(c * chunk, (c + 1) * chunk)
        u = jnp.maximum(_dot(h, wu_ref[:, sl]), 0.0)
        acc = acc + _dot((u * u).astype(BF16), wd_ref[sl, :])
    o_ref[...] = x + _rms(acc, post_g_ref[...])


def _mlp(x2, pre_g, wu_bf, wd_bf, post_g):
    t, d = x2.shape
    d_ff = wu_bf.shape[1]
    tm = min(ROW_TILE, t)
    return pl.pallas_call(
        functools.partial(_mlp_kernel, chunk=min(1024, d_ff)),
        grid=(t // tm,),
        in_specs=[pl.BlockSpec((tm, d), lambda i: (i, 0)),
                  pl.BlockSpec((1, d), lambda i: (0, 0)),
                  pl.BlockSpec((d, d_ff), lambda i: (0, 0)),
                  pl.BlockSpec((d_ff, d), lambda i: (0, 0)),
                  pl.BlockSpec((1, d), lambda i: (0, 0))],
        out_specs=pl.BlockSpec((tm, d), lambda i: (i, 0)),
        out_shape=jax.ShapeDtypeStruct((t, d), F32),
        compiler_params=_params("parallel"),
        name="mlp",
    )(x2, pre_g, wu_bf, wd_bf, post_g)


def _qkv_dil_kernel(x_ref, g_ref, w_ref, cos_ref, sin_ref, o_ref, h_ref, *, scale, n_rope, n_q):
    j = pl.program_id(1)

    @pl.when(j == 0)
    def _():
        h_ref[...] = _rms(x_ref[...], g_ref[...]).astype(BF16)

    y = _dot(h_ref[...], w_ref[...])

    @pl.when(j < n_rope)
    def _():
        cos = cos_ref[...]
        sin = sin_ref[...]
        mult = jnp.where(j < n_q, scale, 1.0).astype(F32)
        for hh in range(y.shape[1] // HEAD_W):
            sl = slice(hh * HEAD_W, (hh + 1) * HEAD_W)
            o_ref[:, sl] = (_rope_slab(y[:, sl], cos, sin, HEAD_W // 2) * mult).astype(BF16)

    @pl.when(j >= n_rope)
    def _():
        o_ref[...] = y.astype(BF16)


def _qkv_dilated(x2, g, w_bf, cos, sin):
    t, d = x2.shape
    n = w_bf.shape[1]
    groups = len(DB_GROUPS)
    tm = min(ROW_TILE, t)
    return pl.pallas_call(
        functools.partial(_qkv_dil_kernel, scale=HEAD_W ** -0.5, n_rope=2 * groups, n_q=groups),
        grid=(t // tm, n // d),
        in_specs=[pl.BlockSpec((tm, d), lambda i, j: (i, 0)),
                  pl.BlockSpec((1, d), lambda i, j: (0, 0)),
                  pl.BlockSpec((d, d), lambda i, j: (0, j)),
                  pl.BlockSpec((tm, LANES), lambda i, j: (i, 0)),
                  pl.BlockSpec((tm, LANES), lambda i, j: (i, 0))],
        out_specs=pl.BlockSpec((tm, d), lambda i, j: (i, j)),
        out_shape=jax.ShapeDtypeStruct((t, n), BF16),
        scratch_shapes=[pltpu.VMEM((tm, d), BF16)],
        compiler_params=_params("parallel", "arbitrary"),
        name="qkv_dilated",
    )(x2, g, w_bf, cos, sin)


def _band_attn_kernel(q_ref, km_ref, kl_ref, kr_ref, vm_ref, vl_ref, vr_ref, o_ref, lse_ref,
                      kc_ref, vc_ref, *, lt, sub_len):
    i = pl.program_id(2)
    r = DB_RADIUS
    kc_ref[0:r] = kl_ref[0]
    kc_ref[r:r + lt] = km_ref[0]
    kc_ref[r + lt:r + lt + r] = kr_ref[0]
    vc_ref[0:r] = vl_ref[0]
    vc_ref[r:r + lt] = vm_ref[0]
    vc_ref[r + lt:r + lt + r] = vr_ref[0]

    qb = BAND_QB
    kb = qb + 2 * r
    qi = lax.broadcasted_iota(jnp.int32, (qb, kb), 0)
    ki = lax.broadcasted_iota(jnp.int32, (qb, kb), 1)
    band = jnp.abs(ki - r - qi) <= r
    lane = lax.broadcasted_iota(jnp.int32, (qb, LANES), 1)

    def q_block(j, carry):
        q0 = pl.multiple_of(j * qb, qb)
        key_pos = i * lt + q0 - r + ki
        mask = band & (key_pos >= 0) & (key_pos < sub_len)
        lse_all = jnp.zeros((qb, LANES), F32)
        for hh in range(DB_HEADS):
            sl = slice(hh * HEAD_W, (hh + 1) * HEAD_W)
            s = _dot_nt(q_ref[0, pl.ds(q0, qb), sl], kc_ref[pl.ds(q0, kb), sl])
            s = jnp.where(mask, s, NEG_BIG)
            m = jnp.max(s, axis=-1, keepdims=True)
            e = jnp.exp(s - m)
            den = jnp.sum(e, axis=-1, keepdims=True)
            p = (e * (1.0 / den)).astype(BF16)
            o_ref[0, pl.ds(q0, qb), sl] = _dot(p, vc_ref[pl.ds(q0, kb), sl]).astype(BF16)
            lse_all = jnp.where(lane == hh, m + jnp.log(den), lse_all)
        lse_ref[0, pl.ds(q0, qb), :] = lse_all
        return carry

    lax.fori_loop(0, lt // qb, q_block, 0)


def _band_attention(qkv3, group, dil):
    batch, seq, n = qkv3.shape
    groups = len(DB_GROUPS)
    d = n // (3 * groups)
    sub_len = seq // dil
    lt = min(BAND_LT, sub_len)
    r = DB_RADIUS
    hpt = lt // r
    n_halo = sub_len // r
    ncol = n // d
    view = qkv3.reshape(batch, sub_len, dil * n)

    def main_spec(which):
        return pl.BlockSpec((1, lt, d), lambda b, rr, i: (b, i, rr * ncol + which * groups + group))

    def left_spec(which):
        return pl.BlockSpec((1, r, d), lambda b, rr, i: (
            b, jnp.maximum(i * hpt - 1, 0), rr * ncol + which * groups + group))

    def right_spec(which):
        return pl.BlockSpec((1, r, d), lambda b, rr, i: (
            b, jnp.minimum((i + 1) * hpt, n_halo - 1), rr * ncol + which * groups + group))

    o, lse = pl.pallas_call(
        functools.partial(_band_attn_kernel, lt=lt, sub_len=sub_len),
        grid=(batch, dil, sub_len // lt),
        in_specs=[main_spec(0), main_spec(1), left_spec(1), right_spec(1),
                  main_spec(2), left_spec(2), right_spec(2)],
        out_specs=[pl.BlockSpec((1, lt, d), lambda b, rr, i: (b, i, rr)),
                   pl.BlockSpec((1, lt, LANES), lambda b, rr, i: (b, i, rr))],
        out_shape=[jax.ShapeDtypeStruct((batch, sub_len, dil * d), BF16),
                   jax.ShapeDtypeStruct((batch, sub_len, dil * LANES), F32)],
        scratch_shapes=[pltpu.VMEM((lt + 2 * r, d), BF16),
                        pltpu.VMEM((lt + 2 * r, d), BF16)],
        compiler_params=_params("parallel", "parallel", "parallel"),
        name=f"band_attention_d{dil}",
    )(view, view, view, view, view, view, view)
    return o.reshape(batch * seq, d), lse.reshape(batch * seq, LANES)


def _combine_proj_kernel(o0_ref, o1_ref, o2_ref, l0_ref, l1_ref, l2_ref, w_ref, g_ref, x_ref,
                         out_ref):
    l0 = l0_ref[...]
    l1 = l1_ref[...]
    l2 = l2_ref[...]
    mx = jnp.maximum(jnp.maximum(l0, l1), l2)
    e0 = jnp.exp(l0 - mx)
    e1 = jnp.exp(l1 - mx)
    e2 = jnp.exp(l2 - mx)
    inv = 1.0 / (e0 + e1 + e2)
    w0 = e0 * inv
    w1 = e1 * inv
    w2 = e2 * inv
    outs = []
    for hh in range(DB_HEADS):
        sl = slice(hh * HEAD_W, (hh + 1) * HEAD_W)
        c = slice(hh, hh + 1)
        oh = (w0[:, c] * o0_ref[:, sl].astype(F32) + w1[:, c] * o1_ref[:, sl].astype(F32)
              + w2[:, c] * o2_ref[:, sl].astype(F32))
        outs.append(oh.astype(BF16))
    y = _dot(jnp.concatenate(outs, axis=1), w_ref[...])
    out_ref[...] = x_ref[...] + _rms(y, g_ref[...])


def _combine_proj(os, lses, w_bf, g, x2):
    t, d = x2.shape
    tm = min(ROW_TILE, t)
    row = lambda i: (i, 0)
    return pl.pallas_call(
        _combine_proj_kernel,
        grid=(t // tm,),
        in_specs=[pl.BlockSpec((tm, d), row)] * 3 + [pl.BlockSpec((tm, LANES), row)] * 3
                 + [pl.BlockSpec(w_bf.shape, lambda i: (0, 0)),
                    pl.BlockSpec((1, d), lambda i: (0, 0)),
                    pl.BlockSpec((tm, d), row)],
        out_specs=pl.BlockSpec((tm, d), row),
        out_shape=jax.ShapeDtypeStruct((t, d), F32),
        compiler_params=_params("parallel"),
        name="combine_proj",
    )(*os, *lses, w_bf, g, x2)


def _row(v):
    return v.reshape(1, -1)


def kernel(x, mem, positions, mix_pre_g, mix_post_g, da_w_qkv, da_lambda, da_subln_g, da_w_o,
           db_w_qkv, db_w_o, xa_pre_g, mem_g, xa_wq, xa_wkv, xa_wo, xa_post_g,
           ffn_pre_g, w_up, w_down, ffn_post_g):
    batch, seq, d = x.shape
    mem_len = mem.shape[1]
    depth = mix_pre_g.shape[0]
    n_mixers = 2
    x2 = x.reshape(batch * seq, d)
    mem2 = mem.reshape(batch * mem_len, d)
    pos_col = positions.astype(F32).reshape(batch * seq, 1)

    for i in range(depth):
        j = i // n_mixers
        if i % n_mixers == 0:
            lambda_init = 0.8 - 0.6 * math.exp(-0.3 * i)
            cos, sin = _rope_tables(pos_col, DA_QK_DIM)
            qt, k2, vt = _qkv_diff(x2, _row(mix_pre_g[i]), da_w_qkv[j].astype(BF16), cos, sin,
                                   batch, seq)
            o = _diff_attention(qt, k2.reshape(batch, seq, d), vt, da_lambda[j],
                                _row(da_subln_g[j]), lambda_init)
            x2 = _proj_residual(o.reshape(batch * seq, d), da_w_o[j].astype(BF16),
                                _row(mix_post_g[i]), x2)
        else:
            cos, sin = _rope_tables(pos_col, HEAD_W)
            qkv = _qkv_dilated(x2, _row(mix_pre_g[i]), db_w_qkv[j].astype(BF16), cos, sin)
            qkv3 = qkv.reshape(batch, seq, qkv.shape[1])
            os, lses = [], []
            for gi, (_, dil) in enumerate(DB_GROUPS):
                o, lse = _band_attention(qkv3, gi, dil)
                os.append(o)
                lses.append(lse)
            x2 = _combine_proj(os, lses, db_w_o[j].astype(BF16), _row(mix_post_g[i]), x2)
        kv = _mem_kv(mem2, _row(mem_g[i]), xa_wkv[i].astype(BF16))
        x2 = _cross_attention(x2, _row(xa_pre_g[i]), xa_wq[i].astype(BF16), kv,
                              xa_wo[i].astype(BF16), _row(xa_post_g[i]), seq, mem_len)
        x2 = _mlp(x2, _row(ffn_pre_g[i]), w_up[i].astype(BF16), w_down[i].astype(BF16),
                  _row(ffn_post_g[i]))
    return x2.reshape(batch, seq, d)
```

```python
import functools
import math

import jax
import jax.numpy as jnp
from jax import lax
from jax.experimental import pallas as pl
from jax.experimental.pallas import tpu as pltpu

F32 = jnp.float32
BF16 = jnp.bfloat16

NORM_EPS = 1e-6
ROPE_THETA = 10000.0
LANES = 128
HEAD_W = 128
DA_HEADS = 8
DA_QK_DIM = 64
DA_VT_ROWS = HEAD_W + 16
DB_GROUPS = ((128, 1), (512, 4), (2048, 16))
DB_HEADS = 8
DB_RADIUS = 64
XA_HEADS = 4
XA_HEAD_DIM = 256
NEG_BIG = -1e30
VMEM_LIMIT = 48 * 1024 * 1024

ROW_TILE = 512
ATT_TQ = 512
ATT_TK = 512
BAND_LT = 512
BAND_QB = 128


def _params(*sem):
    return pltpu.CompilerParams(dimension_semantics=sem, vmem_limit_bytes=VMEM_LIMIT)


def _dot(a, b):
    return jnp.dot(a, b, preferred_element_type=F32)


def _dot_nt(a, b):
    return lax.dot_general(a, b, (((1,), (1,)), ((), ())), preferred_element_type=F32)


def _rms(x, g):
    ms = jnp.mean(x * x, axis=-1, keepdims=True)
    return x * lax.rsqrt(ms + NORM_EPS) * g


def _rope_slab(t, cos, sin, half):
    if half == 64:
        rot = pltpu.roll(t, 64, 1)
    else:
        lane = lax.broadcasted_iota(jnp.int32, t.shape, 1)
        rot = jnp.where((lane & 32) == 0, pltpu.roll(t, 96, 1), pltpu.roll(t, 32, 1))
    return t * cos + rot * sin


def _rope_table_kernel(pos_ref, inv_ref, sign_ref, cos_ref, sin_ref):
    ang = pos_ref[...] * inv_ref[...]
    cos_ref[...] = jnp.cos(ang)
    sin_ref[...] = jnp.sin(ang) * sign_ref[...]


def _rope_tables(pos_col, dh):
    t = pos_col.shape[0]
    inv = ROPE_THETA ** (-jnp.arange(0, dh, 2, dtype=F32) / dh)
    reps = LANES // dh
    inv_full = jnp.tile(jnp.concatenate([inv, inv]), reps).reshape(1, LANES)
    sign = jnp.tile(jnp.concatenate([-jnp.ones(dh // 2, F32), jnp.ones(dh // 2, F32)]),
                    reps).reshape(1, LANES)
    tm = min(t, 2048)
    return pl.pallas_call(
        _rope_table_kernel,
        grid=(t // tm,),
        in_specs=[pl.BlockSpec((tm, 1), lambda i: (i, 0)),
                  pl.BlockSpec((1, LANES), lambda i: (0, 0)),
                  pl.BlockSpec((1, LANES), lambda i: (0, 0))],
        out_specs=[pl.BlockSpec((tm, LANES), lambda i: (i, 0)),
                   pl.BlockSpec((tm, LANES), lambda i: (i, 0))],
        out_shape=[jax.ShapeDtypeStruct((t, LANES), F32)] * 2,
        compiler_params=_params("parallel"),
        name="rope_tables",
    )(pos_col, inv_full, sign)


def _qkv_diff_kernel(x_ref, g_ref, w_ref, cos_ref, sin_ref, qt_ref, k_ref, vt_ref, *, scale):
    d = x_ref.shape[1]
    h = _rms(x_ref[...], g_ref[...]).astype(BF16)
    cos = cos_ref[...]
    sin = sin_ref[...]
    q = _dot(h, w_ref[:, 0:d])
    for hh in range(DA_HEADS):
        sl = slice(hh * HEAD_W, (hh + 1) * HEAD_W)
        t = _rope_slab(q[:, sl], cos, sin, DA_QK_DIM // 2) * scale
        qt_ref[0, 0, sl, :] = t.T.astype(BF16)
    k = _dot(h, w_ref[:, d:2 * d])
    for hh in range(DA_HEADS):
        sl = slice(hh * HEAD_W, (hh + 1) * HEAD_W)
        k_ref[:, sl] = _rope_slab(k[:, sl], cos, sin, DA_QK_DIM // 2).astype(BF16)
    v = _dot(h, w_ref[:, 2 * d:3 * d])
    ones = jnp.ones((DA_VT_ROWS - HEAD_W, x_ref.shape[0]), BF16)
    for hh in range(DA_HEADS):
        sl = slice(hh * HEAD_W, (hh + 1) * HEAD_W)
        r0 = hh * DA_VT_ROWS
        vt_ref[0, 0, r0:r0 + HEAD_W, :] = v[:, sl].T.astype(BF16)
        vt_ref[0, 0, r0 + HEAD_W:r0 + DA_VT_ROWS, :] = ones


def _qkv_diff(x2, g, w_bf, cos, sin, batch, seq):
    t, d = x2.shape
    tm = min(ATT_TQ, seq)
    nst = seq // tm
    scale = DA_QK_DIM ** -0.5 * math.log2(math.e)
    vt_rows = DA_HEADS * DA_VT_ROWS
    return pl.pallas_call(
        functools.partial(_qkv_diff_kernel, scale=scale),
        grid=(t // tm,),
        in_specs=[pl.BlockSpec((tm, d), lambda i: (i, 0)),
                  pl.BlockSpec((1, d), lambda i: (0, 0)),
                  pl.BlockSpec((d, 3 * d), lambda i: (0, 0)),
                  pl.BlockSpec((tm, LANES), lambda i: (i, 0)),
                  pl.BlockSpec((tm, LANES), lambda i: (i, 0))],
        out_specs=[pl.BlockSpec((1, 1, d, tm), lambda i: (i // nst, i % nst, 0, 0)),
                   pl.BlockSpec((tm, d), lambda i: (i, 0)),
                   pl.BlockSpec((1, 1, vt_rows, tm), lambda i: (i // nst, i % nst, 0, 0))],
        out_shape=[jax.ShapeDtypeStruct((batch, nst, d, tm), BF16),
                   jax.ShapeDtypeStruct((t, d), BF16),
                   jax.ShapeDtypeStruct((batch, nst, vt_rows, tm), BF16)],
        compiler_params=_params("parallel"),
        name="qkv_diff",
    )(x2, g, w_bf, cos, sin)


def _diff_attn_kernel(qt_ref, k_ref, vt_ref, lam_ref, g_ref, o_ref,
                      qp_ref, s0_ref, s1_ref, e0_ref, e1_ref, t0_ref, t1_ref, a0_ref, a1_ref,
                      m_ref, acc_ref, *, nk, tk, lambda_init):
    qt = qt_ref[0, 0]
    row = lax.broadcasted_iota(jnp.int32, qt.shape, 0)
    zero = jnp.zeros_like(qt)
    qp_ref[0] = jnp.where(row < DA_QK_DIM, qt, zero)
    qp_ref[1] = jnp.where(row >= DA_QK_DIM, qt, zero)
    m_ref[...] = jnp.full(m_ref.shape, NEG_BIG, F32)
    acc_ref[...] = jnp.zeros(acc_ref.shape, F32)
    e1_ref[...] = jnp.zeros(e1_ref.shape, BF16)
    a1_ref[...] = jnp.zeros(a1_ref.shape, F32)

    def scores(t, s_ref, tmax_ref):
        kb = k_ref[0, pl.ds(pl.multiple_of(t * tk, tk), tk), :]
        for c in range(2):
            s = _dot(kb, qp_ref[c])
            s_ref[c] = s
            tmax_ref[c] = jnp.max(s, axis=0, keepdims=True)

    def exponentiate(s_ref, tmax_ref, e_ref, alpha_ref):
        for c in range(2):
            m_prev = m_ref[c]
            m_new = jnp.maximum(m_prev, tmax_ref[c])
            alpha_ref[c] = jnp.exp2(m_prev - m_new)
            m_ref[c] = m_new
            e_ref[c] = jnp.exp2(s_ref[c] - m_new).astype(BF16)

    def accumulate(t, e_ref, alpha_ref):
        vt = vt_ref[0, t]
        for c in range(2):
            acc_ref[c] = alpha_ref[c] * acc_ref[c] + _dot(vt, e_ref[c])

    scores(0, s0_ref, t0_ref)

    def kv_pair(u, carry):
        t = 2 * u
        scores(t + 1, s1_ref, t1_ref)
        exponentiate(s0_ref, t0_ref, e0_ref, a0_ref)
        accumulate(jnp.maximum(t - 1, 0), e1_ref, a1_ref)
        scores(jnp.minimum(t + 2, nk - 1), s0_ref, t0_ref)
        exponentiate(s1_ref, t1_ref, e1_ref, a1_ref)
        accumulate(t, e0_ref, a0_ref)
        return carry

    lax.fori_loop(0, nk // 2, kv_pair, 0)
    accumulate(nk - 1, e1_ref, a1_ref)

    lam = lam_ref[...]
    lam_full = (jnp.exp(jnp.sum(lam[0:1] * lam[1:2], axis=-1, keepdims=True))
                - jnp.exp(jnp.sum(lam[2:3] * lam[3:4], axis=-1, keepdims=True))
                + lambda_init)
    inv1 = 1.0 / acc_ref[0, HEAD_W:HEAD_W + 1, :]
    inv2 = 1.0 / acc_ref[1, HEAD_W:HEAD_W + 1, :]
    ot = acc_ref[0, 0:HEAD_W, :] * inv1 - lam_full * (acc_ref[1, 0:HEAD_W, :] * inv2)
    o = _rms(ot.T, g_ref[...]) * (1.0 - lambda_init)
    o_ref[0] = o.astype(BF16)


def _diff_attention(qt, k3, vt, lam, subln_g, lambda_init):
    batch, nq, d, tq = qt.shape
    seq = k3.shape[1]
    tk = vt.shape[3]
    nk = vt.shape[1]
    assert nk % 2 == 0
    return pl.pallas_call(
        functools.partial(_diff_attn_kernel, nk=nk, tk=tk, lambda_init=lambda_init),
        grid=(batch, DA_HEADS, nq),
        in_specs=[pl.BlockSpec((1, 1, HEAD_W, tq), lambda b, h, i: (b, i, h, 0)),
                  pl.BlockSpec((1, seq, HEAD_W), lambda b, h, i: (b, 0, h)),
                  pl.BlockSpec((1, nk, DA_VT_ROWS, tk), lambda b, h, i: (b, 0, h, 0)),
                  pl.BlockSpec(lam.shape, lambda b, h, i: (0, 0)),
                  pl.BlockSpec((1, HEAD_W), lambda b, h, i: (0, 0))],
        out_specs=pl.BlockSpec((1, tq, HEAD_W), lambda b, h, i: (b, i, h)),
        out_shape=jax.ShapeDtypeStruct((batch, seq, d), BF16),
        scratch_shapes=[pltpu.VMEM((2, HEAD_W, tq), BF16),
                        pltpu.VMEM((2, tk, tq), F32),
                        pltpu.VMEM((2, tk, tq), F32),
                        pltpu.VMEM((2, tk, tq), BF16),
                        pltpu.VMEM((2, tk, tq), BF16),
                        pltpu.VMEM((2, 1, tq), F32),
                        pltpu.VMEM((2, 1, tq), F32),
                        pltpu.VMEM((2, 1, tq), F32),
                        pltpu.VMEM((2, 1, tq), F32),
                        pltpu.VMEM((2, 1, tq), F32),
                        pltpu.VMEM((2, DA_VT_ROWS, tq), F32)],
        compiler_params=_params("parallel", "parallel", "parallel"),
        name="diff_attention",
    )(qt, k3, vt, lam, subln_g)


def _proj_residual_kernel(a_ref, w_ref, g_ref, x_ref, o_ref):
    y = _dot(a_ref[...], w_ref[...])
    o_ref[...] = x_ref[...] + _rms(y, g_ref[...])


def _proj_residual(a2, w_bf, g, x2):
    t, d = x2.shape
    tm = min(ROW_TILE, t)
    return pl.pallas_call(
        _proj_residual_kernel,
        grid=(t // tm,),
        in_specs=[pl.BlockSpec((tm, a2.shape[1]), lambda i: (i, 0)),
                  pl.BlockSpec(w_bf.shape, lambda i: (0, 0)),
                  pl.BlockSpec((1, d), lambda i: (0, 0)),
                  pl.BlockSpec((tm, d), lambda i: (i, 0))],
        out_specs=pl.BlockSpec((tm, d), lambda i: (i, 0)),
        out_shape=jax.ShapeDtypeStruct((t, d), F32),
        compiler_params=_params("parallel"),
        name="proj_residual",
    )(a2, w_bf, g, x2)


def _mem_kv_kernel(mem_ref, g_ref, w_ref, o_ref):
    mn = _rms(mem_ref[...], g_ref[...]).astype(BF16)
    o_ref[...] = _dot(mn, w_ref[...]).astype(BF16)


def _mem_kv(mem2, g, w_bf):
    rows, d = mem2.shape
    n = w_bf.shape[1]
    tn = n // 2
    return pl.pallas_call(
        _mem_kv_kernel,
        grid=(n // tn,),
        in_specs=[pl.BlockSpec((rows, d), lambda j: (0, 0)),
                  pl.BlockSpec((1, d), lambda j: (0, 0)),
                  pl.BlockSpec((d, tn), lambda j: (0, j))],
        out_specs=pl.BlockSpec((rows, tn), lambda j: (0, j)),
        out_shape=jax.ShapeDtypeStruct((rows, n), BF16),
        compiler_params=_params("parallel"),
        name="mem_kv",
    )(mem2, g, w_bf)


def _cross_attn_kernel(x_ref, pre_g_ref, wq_ref, k_ref, v_ref, wo_ref, post_g_ref, o_ref):
    x = x_ref[...]
    h = _rms(x, pre_g_ref[...]).astype(BF16)
    q = (_dot(h, wq_ref[...]) * (XA_HEAD_DIM ** -0.5)).astype(BF16)
    outs = []
    for hh in range(XA_HEADS):
        sl = slice(hh * XA_HEAD_DIM, (hh + 1) * XA_HEAD_DIM)
        s = _dot_nt(q[:, sl], k_ref[:, sl])
        e = jnp.exp(s - jnp.max(s, axis=-1, keepdims=True))
        p = e * (1.0 / jnp.sum(e, axis=-1, keepdims=True))
        outs.append(_dot(p.astype(BF16), v_ref[:, sl]).astype(BF16))
    o = jnp.concatenate(outs, axis=1)
    y = _dot(o, wo_ref[...])
    o_ref[...] = x + _rms(y, post_g_ref[...])


def _cross_attention(x2, pre_g, wq_bf, kv, wo_bf, post_g, seq, mem_len):
    t, d = x2.shape
    tm = min(ROW_TILE, seq)
    nst = seq // tm
    return pl.pallas_call(
        _cross_attn_kernel,
        grid=(t // tm,),
        in_specs=[pl.BlockSpec((tm, d), lambda i: (i, 0)),
                  pl.BlockSpec((1, d), lambda i: (0, 0)),
                  pl.BlockSpec((d, d), lambda i: (0, 0)),
                  pl.BlockSpec((mem_len, d), lambda i: (i // nst, 0)),
                  pl.BlockSpec((mem_len, d), lambda i: (i // nst, 1)),
                  pl.BlockSpec((d, d), lambda i: (0, 0)),
                  pl.BlockSpec((1, d), lambda i: (0, 0))],
        out_specs=pl.BlockSpec((tm, d), lambda i: (i, 0)),
        out_shape=jax.ShapeDtypeStruct((t, d), F32),
        compiler_params=_params("parallel"),
        name="cross_attention",
    )(x2, pre_g, wq_bf, kv, kv, wo_bf, post_g)


def _mlp_kernel(x_ref, pre_g_ref, wu_ref, wd_ref, post_g_ref, o_ref, *, chunk):
    x = x_ref[...]
    h = _rms(x, pre_g_ref[...]).astype(BF16)
    d_ff = wu_ref.shape[1]
    acc = jnp.zeros(x.shape, F32)
    for c in range(d_ff // chunk):
        sl = slice(c * chunk, (c + 1) * chunk)
        u = jnp.maximum(_dot(h, wu_ref[:, sl]), 0.0)
        acc = acc + _dot((u * u).astype(BF16), wd_ref[sl, :])
    o_ref[...] = x + _rms(acc, post_g_ref[...])


def _mlp(x2, pre_g, wu_bf, wd_bf, post_g):
    t, d = x2.shape
    d_ff = wu_bf.shape[1]
    tm = min(ROW_TILE, t)
    return pl.pallas_call(
        functools.partial(_mlp_kernel, chunk=min(1024, d_ff)),
        grid=(t // tm,),
        in_specs=[pl.BlockSpec((tm, d), lambda i: (i, 0)),
                  pl.BlockSpec((1, d), lambda i: (0, 0)),
                  pl.BlockSpec((d, d_ff), lambda i: (0, 0)),
                  pl.BlockSpec((d_ff, d), lambda i: (0, 0)),
                  pl.BlockSpec((1, d), lambda i: (0, 0))],
        out_specs=pl.BlockSpec((tm, d), lambda i: (i, 0)),
        out_shape=jax.ShapeDtypeStruct((t, d), F32),
        compiler_params=_params("parallel"),
        name="mlp",
    )(x2, pre_g, wu_bf, wd_bf, post_g)


def _qkv_dil_kernel(x_ref, g_ref, w_ref, cos_ref, sin_ref, o_ref, *stage, scale, dil):
    tm, d = x_ref.shape
    h = _rms(x_ref[...], g_ref[...]).astype(BF16)
    cos = cos_ref[...]
    sin = sin_ref[...]
    for which in range(3):
        y = _dot(h, w_ref[:, which * d:(which + 1) * d])
        for hh in range(d // HEAD_W):
            sl = slice(hh * HEAD_W, (hh + 1) * HEAD_W)
            col = slice(which * d + hh * HEAD_W, which * d + (hh + 1) * HEAD_W)
            t = y[:, sl]
            if which < 2:
                t = _rope_slab(t, cos, sin, HEAD_W // 2)
            if which == 0:
                t = t * scale
            if dil == 1:
                o_ref[0, 0, :, col] = t.astype(BF16)
            else:
                stage[which][hh] = t
        if dil > 1:
            for hh in range(d // HEAD_W):
                col = slice(which * d + hh * HEAD_W, which * d + (hh + 1) * HEAD_W)
                for r in range(dil):
                    rows = stage[which][hh, pl.ds(r, tm // dil, stride=dil), :]
                    o_ref[0, r, :, col] = rows.astype(BF16)


def _qkv_dilated(x2, g, w_bf, cos, sin, batch, seq, dil):
    t, d = x2.shape
    tm = min(ROW_TILE, seq)
    nst = seq // tm
    lt = tm // dil
    stage = [pltpu.VMEM((d // HEAD_W, tm, HEAD_W), F32)] * 3 if dil > 1 else []
    return pl.pallas_call(
        functools.partial(_qkv_dil_kernel, scale=HEAD_W ** -0.5, dil=dil),
        grid=(t // tm,),
        in_specs=[pl.BlockSpec((tm, d), lambda i: (i, 0)),
                  pl.BlockSpec((1, d), lambda i: (0, 0)),
                  pl.BlockSpec((d, 3 * d), lambda i: (0, 0)),
                  pl.BlockSpec((tm, LANES), lambda i: (i, 0)),
                  pl.BlockSpec((tm, LANES), lambda i: (i, 0))],
        out_specs=pl.BlockSpec((1, dil, lt, 3 * d), lambda i: (i // nst, 0, i % nst, 0)),
        out_shape=jax.ShapeDtypeStruct((batch, dil, seq // dil, 3 * d), BF16),
        scratch_shapes=stage,
        compiler_params=_params("parallel"),
        name=f"qkv_dilated_d{dil}",
    )(x2, g, w_bf, cos, sin)


def _band_attn_kernel(q_ref, km_ref, kl_ref, kr_ref, vm_ref, vl_ref, vr_ref, o_ref, lse_ref,
                      kc_ref, vc_ref, *, lt, sub_len):
    i = pl.program_id(2)
    r = DB_RADIUS
    kc_ref[0:r] = kl_ref[0, 0]
    kc_ref[r:r + lt] = km_ref[0, 0]
    kc_ref[r + lt:r + lt + r] = kr_ref[0, 0]
    vc_ref[0:r] = vl_ref[0, 0]
    vc_ref[r:r + lt] = vm_ref[0, 0]
    vc_ref[r + lt:r + lt + r] = vr_ref[0, 0]

    qb = BAND_QB
    kb = qb + 2 * r
    qi = lax.broadcasted_iota(jnp.int32, (qb, kb), 0)
    ki = lax.broadcasted_iota(jnp.int32, (qb, kb), 1)
    band = jnp.abs(ki - r - qi) <= r
    lane = lax.broadcasted_iota(jnp.int32, (qb, LANES), 1)

    def q_block(j, carry):
        q0 = pl.multiple_of(j * qb, qb)
        key_pos = i * lt + q0 - r + ki
        mask = band & (key_pos >= 0) & (key_pos < sub_len)
        lse_all = jnp.zeros((qb, LANES), F32)
        for hh in range(DB_HEADS):
            sl = slice(hh * HEAD_W, (hh + 1) * HEAD_W)
            s = _dot_nt(q_ref[0, 0, pl.ds(q0, qb), sl], kc_ref[pl.ds(q0, kb), sl])
            s = jnp.where(mask, s, NEG_BIG)
            m = jnp.max(s, axis=-1, keepdims=True)
            e = jnp.exp(s - m)
            den = jnp.sum(e, axis=-1, keepdims=True)
            p = (e * (1.0 / den)).astype(BF16)
            o_ref[0, 0, pl.ds(q0, qb), sl] = _dot(p, vc_ref[pl.ds(q0, kb), sl]).astype(BF16)
            lse_all = jnp.where(lane == hh, m + jnp.log(den), lse_all)
        lse_ref[0, 0, pl.ds(q0, qb), :] = lse_all
        return carry

    lax.fori_loop(0, lt // qb, q_block, 0)


def _band_attention(qkv4):
    batch, dil, sub_len, n = qkv4.shape
    d = n // 3
    lt = min(BAND_LT, sub_len)
    r = DB_RADIUS
    hpt = lt // r
    n_halo = sub_len // r

    def main_spec(which):
        return pl.BlockSpec((1, 1, lt, d), lambda b, rr, i: (b, rr, i, which))

    def left_spec(which):
        return pl.BlockSpec((1, 1, r, d), lambda b, rr, i: (
            b, rr, jnp.maximum(i * hpt - 1, 0), which))

    def right_spec(which):
        return pl.BlockSpec((1, 1, r, d), lambda b, rr, i: (
            b, rr, jnp.minimum((i + 1) * hpt, n_halo - 1), which))

    return pl.pallas_call(
        functools.partial(_band_attn_kernel, lt=lt, sub_len=sub_len),
        grid=(batch, dil, sub_len // lt),
        in_specs=[main_spec(0), main_spec(1), left_spec(1), right_spec(1),
                  main_spec(2), left_spec(2), right_spec(2)],
        out_specs=[pl.BlockSpec((1, 1, lt, d), lambda b, rr, i: (b, rr, i, 0)),
                   pl.BlockSpec((1, 1, lt, LANES), lambda b, rr, i: (b, rr, i, 0))],
        out_shape=[jax.ShapeDtypeStruct((batch, dil, sub_len, d), BF16),
                   jax.ShapeDtypeStruct((batch, dil, sub_len, LANES), F32)],
        scratch_shapes=[pltpu.VMEM((lt + 2 * r, d), BF16),
                        pltpu.VMEM((lt + 2 * r, d), BF16)],
        compiler_params=_params("parallel", "parallel", "parallel"),
        name=f"band_attention_d{dil}",
    )(qkv4, qkv4, qkv4, qkv4, qkv4, qkv4, qkv4)


def _combine_proj_kernel(o0_ref, o1_ref, o2_ref, l0_ref, l1_ref, l2_ref, w_ref, g_ref, x_ref,
                         out_ref, *stage, dils):
    tm = x_ref.shape[0]
    n_slab = x_ref.shape[1] // HEAD_W
    o_refs = (o0_ref, o1_ref, o2_ref)
    l_refs = (l0_ref, l1_ref, l2_ref)
    stage = list(stage)
    o_tok, l_tok = [], []
    for gi, dil in enumerate(dils):
        if dil == 1:
            o_tok.append(None)
            l_tok.append(l_refs[gi][0, 0])
            continue
        so_ref = stage.pop(0)
        sl_ref = stage.pop(0)
        for r in range(dil):
            rows = pl.ds(r, tm // dil, stride=dil)
            sl_ref[rows, :] = l_refs[gi][0, r]
            for hh in range(n_slab):
                so_ref[hh, rows, :] = o_refs[gi][0, r, :, hh * HEAD_W:(hh + 1) * HEAD_W].astype(F32)
        o_tok.append(so_ref)
        l_tok.append(sl_ref[...])
    mx = jnp.maximum(jnp.maximum(l_tok[0], l_tok[1]), l_tok[2])
    es = [jnp.exp(l - mx) for l in l_tok]
    inv = 1.0 / (es[0] + es[1] + es[2])
    ws = [e * inv for e in es]
    outs = []
    for hh in range(n_slab):
        sl = slice(hh * HEAD_W, (hh + 1) * HEAD_W)
        c = slice(hh, hh + 1)
        oh = None
        for gi, dil in enumerate(dils):
            og = o_refs[gi][0, 0, :, sl].astype(F32) if dil == 1 else o_tok[gi][hh]
            term = ws[gi][:, c] * og
            oh = term if oh is None else oh + term
        outs.append(oh.astype(BF16))
    y = _dot(jnp.concatenate(outs, axis=1), w_ref[...])
    out_ref[...] = x_ref[...] + _rms(y, g_ref[...])


def _combine_proj(os, lses, w_bf, g, x2, seq):
    t, d = x2.shape
    tm = min(ROW_TILE, seq)
    nst = seq // tm
    dils = tuple(o.shape[1] for o in os)
    row = lambda i: (i, 0)

    def group_spec(dil, width):
        return pl.BlockSpec((1, dil, tm // dil, width), lambda i: (i // nst, 0, i % nst, 0))

    stage = []
    for dil in dils:
        if dil > 1:
            stage += [pltpu.VMEM((d // HEAD_W, tm, HEAD_W), F32), pltpu.VMEM((tm, LANES), F32)]
    return pl.pallas_call(
        functools.partial(_combine_proj_kernel, dils=dils),
        grid=(t // tm,),
        in_specs=[group_spec(dil, d) for dil in dils] + [group_spec(dil, LANES) for dil in dils]
                 + [pl.BlockSpec(w_bf.shape, lambda i: (0, 0)),
                    pl.BlockSpec((1, d), lambda i: (0, 0)),
                    pl.BlockSpec((tm, d), row)],
        out_specs=pl.BlockSpec((tm, d), row),
        out_shape=jax.ShapeDtypeStruct((t, d), F32),
        scratch_shapes=stage,
        compiler_params=_params("parallel"),
        name="combine_proj",
    )(*os, *lses, w_bf, g, x2)


def _row(v):
    return v.reshape(1, -1)


def kernel(x, mem, positions, mix_pre_g, mix_post_g, da_w_qkv, da_lambda, da_subln_g, da_w_o,
           db_w_qkv, db_w_o, xa_pre_g, mem_g, xa_wq, xa_wkv, xa_wo, xa_post_g,
           ffn_pre_g, w_up, w_down, ffn_post_g):
    batch, seq, d = x.shape
    mem_len = mem.shape[1]
    depth = mix_pre_g.shape[0]
    n_mixers = 2
    x2 = x.reshape(batch * seq, d)
    mem2 = mem.reshape(batch * mem_len, d)
    pos_col = positions.astype(F32).reshape(batch * seq, 1)

    for i in range(depth):
        j = i // n_mixers
        if i % n_mixers == 0:
            lambda_init = 0.8 - 0.6 * math.exp(-0.3 * i)
            cos, sin = _rope_tables(pos_col, DA_QK_DIM)
            qt, k2, vt = _qkv_diff(x2, _row(mix_pre_g[i]), da_w_qkv[j].astype(BF16), cos, sin,
                                   batch, seq)
            o = _diff_attention(qt, k2.reshape(batch, seq, d), vt, da_lambda[j],
                                _row(da_subln_g[j]), lambda_init)
            x2 = _proj_residual(o.reshape(batch * seq, d), da_w_o[j].astype(BF16),
                                _row(mix_post_g[i]), x2)
        else:
            cos, sin = _rope_tables(pos_col, HEAD_W)
            groups = len(DB_GROUPS)
            w4 = db_w_qkv[j].astype(BF16).reshape(d, 3, groups, d)
            os, lses = [], []
            for gi, (_, dil) in enumerate(DB_GROUPS):
                qkv4 = _qkv_dilated(x2, _row(mix_pre_g[i]), w4[:, :, gi, :].reshape(d, 3 * d),
                                    cos, sin, batch, seq, dil)
                o, lse = _band_attention(qkv4)
                os.append(o)
                lses.append(lse)
            x2 = _combine_proj(os, lses, db_w_o[j].astype(BF16), _row(mix_post_g[i]), x2, seq)
        kv = _mem_kv(mem2, _row(mem_g[i]), xa_wkv[i].astype(BF16))
        x2 = _cross_attention(x2, _row(xa_pre_g[i]), xa_wq[i].astype(BF16), kv,
                              xa_wo[i].astype(BF16), _row(xa_post_g[i]), seq, mem_len)
        x2 = _mlp(x2, _row(ffn_pre_g[i]), w_up[i].astype(BF16), w_down[i].astype(BF16),
                  _row(ffn_post_g[i]))
    return x2.reshape(batch, seq, d)
```

```python
import functools
import math

import jax
import jax.numpy as jnp
from jax import lax
from jax.experimental import pallas as pl
from jax.experimental.pallas import tpu as pltpu

F32 = jnp.float32
BF16 = jnp.bfloat16

NORM_EPS = 1e-6
ROPE_THETA = 10000.0
LANES = 128
HEAD_W = 128
DA_HEADS = 8
DA_QK_DIM = 64
DA_VT_ROWS = HEAD_W + 16
DB_GROUPS = ((128, 1), (512, 4), (2048, 16))
DB_HEADS = 8
DB_RADIUS = 64
XA_HEADS = 4
XA_HEAD_DIM = 256
NEG_BIG = -1e30
VMEM_LIMIT = 48 * 1024 * 1024
TAIL_VMEM_LIMIT = 56 * 1024 * 1024
MLP_CHUNK = 1024

ROW_TILE = 512
ATT_TQ = 2048
ATT_TK = 512
BAND_LT = 512
BAND_QB = 128


def _params(*sem):
    return pltpu.CompilerParams(dimension_semantics=sem, vmem_limit_bytes=VMEM_LIMIT)


def _dot(a, b):
    return jnp.dot(a, b, preferred_element_type=F32)


def _dot_nt(a, b):
    return lax.dot_general(a, b, (((1,), (1,)), ((), ())), preferred_element_type=F32)


def _rms(x, g):
    ms = jnp.mean(x * x, axis=-1, keepdims=True)
    return x * lax.rsqrt(ms + NORM_EPS) * g


def _rope_slab(t, cos, sin, half):
    if half == 64:
        rot = pltpu.roll(t, 64, 1)
    else:
        lane = lax.broadcasted_iota(jnp.int32, t.shape, 1)
        rot = jnp.where((lane & 32) == 0, pltpu.roll(t, 96, 1), pltpu.roll(t, 32, 1))
    return t * cos + rot * sin


def _rope_table_kernel(pos_ref, inv_ref, sign_ref, cos_ref, sin_ref):
    ang = pos_ref[...] * inv_ref[...]
    cos_ref[...] = jnp.cos(ang)
    sin_ref[...] = jnp.sin(ang) * sign_ref[...]


def _rope_tables(pos_col, dh):
    t = pos_col.shape[0]
    inv = ROPE_THETA ** (-jnp.arange(0, dh, 2, dtype=F32) / dh)
    reps = LANES // dh
    inv_full = jnp.tile(jnp.concatenate([inv, inv]), reps).reshape(1, LANES)
    sign = jnp.tile(jnp.concatenate([-jnp.ones(dh // 2, F32), jnp.ones(dh // 2, F32)]),
                    reps).reshape(1, LANES)
    tm = min(t, 2048)
    return pl.pallas_call(
        _rope_table_kernel,
        grid=(t // tm,),
        in_specs=[pl.BlockSpec((tm, 1), lambda i: (i, 0)),
                  pl.BlockSpec((1, LANES), lambda i: (0, 0)),
                  pl.BlockSpec((1, LANES), lambda i: (0, 0))],
        out_specs=[pl.BlockSpec((tm, LANES), lambda i: (i, 0)),
                   pl.BlockSpec((tm, LANES), lambda i: (i, 0))],
        out_shape=[jax.ShapeDtypeStruct((t, LANES), F32)] * 2,
        compiler_params=_params("parallel"),
        name="rope_tables",
    )(pos_col, inv_full, sign)


def _qkv_diff_kernel(x_ref, g_ref, w_ref, cos_ref, sin_ref, qt_ref, k_ref, vt_ref, *, scale):
    d = x_ref.shape[1]
    h = _rms(x_ref[...], g_ref[...]).astype(BF16)
    cos = cos_ref[...]
    sin = sin_ref[...]
    q = _dot(h, w_ref[:, 0:d])
    for hh in range(DA_HEADS):
        sl = slice(hh * HEAD_W, (hh + 1) * HEAD_W)
        t = _rope_slab(q[:, sl], cos, sin, DA_QK_DIM // 2) * scale
        qt_ref[0, 0, sl, :] = t.T.astype(BF16)
    k = _dot(h, w_ref[:, d:2 * d])
    for hh in range(DA_HEADS):
        sl = slice(hh * HEAD_W, (hh + 1) * HEAD_W)
        k_ref[:, sl] = _rope_slab(k[:, sl], cos, sin, DA_QK_DIM // 2).astype(BF16)
    v = _dot(h, w_ref[:, 2 * d:3 * d])
    ones = jnp.ones((DA_VT_ROWS - HEAD_W, x_ref.shape[0]), BF16)
    for hh in range(DA_HEADS):
        sl = slice(hh * HEAD_W, (hh + 1) * HEAD_W)
        r0 = hh * DA_VT_ROWS
        vt_ref[0, 0, r0:r0 + HEAD_W, :] = v[:, sl].T.astype(BF16)
        vt_ref[0, 0, r0 + HEAD_W:r0 + DA_VT_ROWS, :] = ones


def _qkv_diff(x2, g, w_bf, cos, sin, batch, seq):
    t, d = x2.shape
    tm = min(ATT_TK, seq)
    nst = seq // tm
    scale = DA_QK_DIM ** -0.5 * math.log2(math.e)
    vt_rows = DA_HEADS * DA_VT_ROWS
    return pl.pallas_call(
        functools.partial(_qkv_diff_kernel, scale=scale),
        grid=(t // tm,),
        in_specs=[pl.BlockSpec((tm, d), lambda i: (i, 0)),
                  pl.BlockSpec((1, d), lambda i: (0, 0)),
                  pl.BlockSpec((d, 3 * d), lambda i: (0, 0)),
                  pl.BlockSpec((tm, LANES), lambda i: (i, 0)),
                  pl.BlockSpec((tm, LANES), lambda i: (i, 0))],
        out_specs=[pl.BlockSpec((1, 1, d, tm), lambda i: (i // nst, i % nst, 0, 0)),
                   pl.BlockSpec((tm, d), lambda i: (i, 0)),
                   pl.BlockSpec((1, 1, vt_rows, tm), lambda i: (i // nst, i % nst, 0, 0))],
        out_shape=[jax.ShapeDtypeStruct((batch, nst, d, tm), BF16),
                   jax.ShapeDtypeStruct((t, d), BF16),
                   jax.ShapeDtypeStruct((batch, nst, vt_rows, tm), BF16)],
        compiler_params=_params("parallel"),
        name="qkv_diff",
    )(x2, g, w_bf, cos, sin)


def _diff_attn_kernel(qt_ref, k_ref, vt_ref, lam_ref, g_ref, o_ref,
                      qp_ref, s0_ref, s1_ref, e0_ref, e1_ref, t0_ref, t1_ref, a0_ref, a1_ref,
                      m_ref, acc_ref, *, nk, tk, lambda_init):
    qt = jnp.concatenate([qt_ref[0, j] for j in range(qt_ref.shape[1])], axis=1)
    row = lax.broadcasted_iota(jnp.int32, qt.shape, 0)
    zero = jnp.zeros_like(qt)
    qp_ref[0] = jnp.where(row < DA_QK_DIM, qt, zero)
    qp_ref[1] = jnp.where(row >= DA_QK_DIM, qt, zero)
    m_ref[...] = jnp.full(m_ref.shape, NEG_BIG, F32)
    acc_ref[...] = jnp.zeros(acc_ref.shape, F32)
    e1_ref[...] = jnp.zeros(e1_ref.shape, BF16)
    a1_ref[...] = jnp.zeros(a1_ref.shape, F32)

    def scores(t, s_ref, tmax_ref):
        kb = k_ref[0, pl.ds(pl.multiple_of(t * tk, tk), tk), :]
        for c in range(2):
            s = _dot(kb, qp_ref[c])
            s_ref[c] = s
            tmax_ref[c] = jnp.max(s, axis=0, keepdims=True)

    def exponentiate(s_ref, tmax_ref, e_ref, alpha_ref):
        for c in range(2):
            m_prev = m_ref[c]
            m_new = jnp.maximum(m_prev, tmax_ref[c])
            alpha_ref[c] = jnp.exp2(m_prev - m_new)
            m_ref[c] = m_new
            e_ref[c] = jnp.exp2(s_ref[c] - m_new).astype(BF16)

    def accumulate(t, e_ref, alpha_ref):
        vt = vt_ref[0, t]
        for c in range(2):
            acc_ref[c] = alpha_ref[c] * acc_ref[c] + _dot(vt, e_ref[c])

    scores(0, s0_ref, t0_ref)

    def kv_pair(u, carry):
        t = 2 * u
        scores(t + 1, s1_ref, t1_ref)
        exponentiate(s0_ref, t0_ref, e0_ref, a0_ref)
        accumulate(jnp.maximum(t - 1, 0), e1_ref, a1_ref)
        scores(jnp.minimum(t + 2, nk - 1), s0_ref, t0_ref)
        exponentiate(s1_ref, t1_ref, e1_ref, a1_ref)
        accumulate(t, e0_ref, a0_ref)
        return carry

    lax.fori_loop(0, nk // 2, kv_pair, 0)
    accumulate(nk - 1, e1_ref, a1_ref)

    lam = lam_ref[...]
    lam_full = (jnp.exp(jnp.sum(lam[0:1] * lam[1:2], axis=-1, keepdims=True))
                - jnp.exp(jnp.sum(lam[2:3] * lam[3:4], axis=-1, keepdims=True))
                + lambda_init)
    inv1 = 1.0 / acc_ref[0, HEAD_W:HEAD_W + 1, :]
    inv2 = 1.0 / acc_ref[1, HEAD_W:HEAD_W + 1, :]
    ot = acc_ref[0, 0:HEAD_W, :] * inv1 - lam_full * (acc_ref[1, 0:HEAD_W, :] * inv2)
    o = _rms(ot.T, g_ref[...]) * (1.0 - lambda_init)
    o_ref[0] = o.astype(BF16)


def _diff_attention(qt, k3, vt, lam, subln_g, lambda_init):
    batch, nk, d, tk = qt.shape
    seq = k3.shape[1]
    tq = min(ATT_TQ, seq)
    nqt = tq // tk
    nq = seq // tq
    assert nk % 2 == 0 and tq % tk == 0
    return pl.pallas_call(
        functools.partial(_diff_attn_kernel, nk=nk, tk=tk, lambda_init=lambda_init),
        grid=(batch, DA_HEADS, nq),
        in_specs=[pl.BlockSpec((1, nqt, HEAD_W, tk), lambda b, h, i: (b, i, h, 0)),
                  pl.BlockSpec((1, seq, HEAD_W), lambda b, h, i: (b, 0, h)),
                  pl.BlockSpec((1, nk, DA_VT_ROWS, tk), lambda b, h, i: (b, 0, h, 0)),
                  pl.BlockSpec(lam.shape, lambda b, h, i: (0, 0)),
                  pl.BlockSpec((1, HEAD_W), lambda b, h, i: (0, 0))],
        out_specs=pl.BlockSpec((1, tq, HEAD_W), lambda b, h, i: (b, i, h)),
        out_shape=jax.ShapeDtypeStruct((batch, seq, d), BF16),
        scratch_shapes=[pltpu.VMEM((2, HEAD_W, tq), BF16),
                        pltpu.VMEM((2, tk, tq), F32),
                        pltpu.VMEM((2, tk, tq), F32),
                        pltpu.VMEM((2, tk, tq), BF16),
                        pltpu.VMEM((2, tk, tq), BF16),
                        pltpu.VMEM((2, 1, tq), F32),
                        pltpu.VMEM((2, 1, tq), F32),
                        pltpu.VMEM((2, 1, tq), F32),
                        pltpu.VMEM((2, 1, tq), F32),
                        pltpu.VMEM((2, 1, tq), F32),
                        pltpu.VMEM((2, DA_VT_ROWS, tq), F32)],
        compiler_params=_params("parallel", "parallel", "parallel"),
        name="diff_attention",
    )(qt, k3, vt, lam, subln_g)


def _mem_kv_kernel(mem_ref, g_ref, w_ref, o_ref):
    mn = _rms(mem_ref[...], g_ref[...]).astype(BF16)
    o_ref[...] = _dot(mn, w_ref[...]).astype(BF16)


def _mem_kv(mem2, g, w_bf):
    rows, d = mem2.shape
    n = w_bf.shape[1]
    tn = n // 2
    return pl.pallas_call(
        _mem_kv_kernel,
        grid=(n // tn,),
        in_specs=[pl.BlockSpec((rows, d), lambda j: (0, 0)),
                  pl.BlockSpec((1, d), lambda j: (0, 0)),
                  pl.BlockSpec((d, tn), lambda j: (0, j))],
        out_specs=pl.BlockSpec((rows, tn), lambda j: (0, j)),
        out_shape=jax.ShapeDtypeStruct((rows, n), BF16),
        compiler_params=_params("parallel"),
        name="mem_kv",
    )(mem2, g, w_bf)


def _cross_attn_block(x, pre_g_ref, wq_ref, k_ref, v_ref, wo_ref, post_g_ref):
    h = _rms(x, pre_g_ref[...]).astype(BF16)
    q = (_dot(h, wq_ref[...]) * (XA_HEAD_DIM ** -0.5)).astype(BF16)
    outs = []
    for hh in range(XA_HEADS):
        sl = slice(hh * XA_HEAD_DIM, (hh + 1) * XA_HEAD_DIM)
        s = _dot_nt(q[:, sl], k_ref[:, sl])
        e = jnp.exp(s - jnp.max(s, axis=-1, keepdims=True))
        p = e * (1.0 / jnp.sum(e, axis=-1, keepdims=True))
        outs.append(_dot(p.astype(BF16), v_ref[:, sl]).astype(BF16))
    y = _dot(jnp.concatenate(outs, axis=1), wo_ref[...])
    return x + _rms(y, post_g_ref[...])


def _mlp_block(x, pre_g_ref, wu_ref, wd_ref, post_g_ref):
    h = _rms(x, pre_g_ref[...]).astype(BF16)
    d_ff = wu_ref.shape[1]
    chunk = min(MLP_CHUNK, d_ff)
    acc = jnp.zeros(x.shape, F32)
    for c in range(d_ff // chunk):
        sl = slice(c * chunk, (c + 1) * chunk)
        u = jnp.maximum(_dot(h, wu_ref[:, sl]), 0.0)
        acc = acc + _dot((u * u).astype(BF16), wd_ref[sl, :])
    return x + _rms(acc, post_g_ref[...])


N_XA_REFS = 6
N_FF_REFS = 4


def _finish_layer(a, w_ref, g_ref, x_ref, rest):
    xa_refs = rest[:N_XA_REFS]
    ff_refs = rest[N_XA_REFS:N_XA_REFS + N_FF_REFS]
    out_ref = rest[N_XA_REFS + N_FF_REFS]
    x = x_ref[...] + _rms(_dot(a, w_ref[...]), g_ref[...])
    x = _cross_attn_block(x, *xa_refs)
    out_ref[...] = _mlp_block(x, *ff_refs)


def _tail_kernel(a_ref, w_ref, g_ref, x_ref, *rest):
    _finish_layer(a_ref[...], w_ref, g_ref, x_ref, rest)


def _tail_specs(tm, d, d_ff, mem_len, nst):
    def const(*shape):
        return pl.BlockSpec(shape, lambda i: (0,) * len(shape), pipeline_mode=pl.Buffered(1))

    row = pl.BlockSpec((tm, d), lambda i: (i, 0))
    in_specs = [const(d, d), const(1, d), row,
                const(1, d), const(d, d),
                pl.BlockSpec((mem_len, d), lambda i: (i // nst, 0)),
                pl.BlockSpec((mem_len, d), lambda i: (i // nst, 1)),
                const(d, d), const(1, d),
                const(1, d), const(d, d_ff), const(d_ff, d), const(1, d)]
    return in_specs, row


def _layer_tail(a2, w_o, mix_g, x2, xa, ff, seq, mem_len):
    t, d = x2.shape
    tm = min(ROW_TILE, seq)
    in_specs, row = _tail_specs(tm, d, ff[1].shape[1], mem_len, seq // tm)
    pre_g, wq, kv, wo, post_g = xa
    return pl.pallas_call(
        _tail_kernel,
        grid=(t // tm,),
        in_specs=[row] + in_specs,
        out_specs=row,
        out_shape=jax.ShapeDtypeStruct((t, d), F32),
        compiler_params=pltpu.CompilerParams(dimension_semantics=("parallel",),
                                             vmem_limit_bytes=TAIL_VMEM_LIMIT),
        name="layer_tail",
    )(a2, w_o, mix_g, x2, pre_g, wq, kv, kv, wo, post_g, *ff)


def _qkv_dil_kernel(x_ref, g_ref, w_ref, cos_ref, sin_ref, o_ref, *stage, scale, dil):
    tm, d = x_ref.shape
    lt = tm // dil
    n_slab = d // HEAD_W
    hn = _rms(x_ref[...], g_ref[...])
    if dil == 1:
        h = hn.astype(BF16)
        cos = cos_ref[...]
        sin = sin_ref[...]
    else:
        f_ref, hp_ref, tp_ref = stage
        for hh in range(n_slab):
            f_ref[hh] = hn[:, hh * HEAD_W:(hh + 1) * HEAD_W]
        f_ref[n_slab] = cos_ref[...]
        f_ref[n_slab + 1] = sin_ref[...]
        for r in range(dil):
            src = pl.ds(r, lt, stride=dil)
            dst = slice(r * lt, (r + 1) * lt)
            for hh in range(n_slab):
                hp_ref[dst, hh * HEAD_W:(hh + 1) * HEAD_W] = f_ref[hh, src, :].astype(BF16)
            tp_ref[0, dst, :] = f_ref[n_slab, src, :]
            tp_ref[1, dst, :] = f_ref[n_slab + 1, src, :]
        h = hp_ref[...]
        cos = tp_ref[0]
        sin = tp_ref[1]
    for which in range(3):
        y = _dot(h, w_ref[:, which * d:(which + 1) * d])
        for hh in range(n_slab):
            col = slice(which * d + hh * HEAD_W, which * d + (hh + 1) * HEAD_W)
            t = y[:, hh * HEAD_W:(hh + 1) * HEAD_W]
            if which < 2:
                t = _rope_slab(t, cos, sin, HEAD_W // 2)
            if which == 0:
                t = t * scale
            t = t.astype(BF16)
            for r in range(dil):
                o_ref[0, r, :, col] = t[r * lt:(r + 1) * lt]


def _qkv_dilated(x2, g, w_bf, cos, sin, batch, seq, dil):
    t, d = x2.shape
    tm = min(ROW_TILE, seq)
    nst = seq // tm
    lt = tm // dil
    stage = []
    if dil > 1:
        stage = [pltpu.VMEM((d // HEAD_W + 2, tm, HEAD_W), F32),
                 pltpu.VMEM((tm, d), BF16),
                 pltpu.VMEM((2, tm, HEAD_W), F32)]
    return pl.pallas_call(
        functools.partial(_qkv_dil_kernel, scale=HEAD_W ** -0.5, dil=dil),
        grid=(t // tm,),
        in_specs=[pl.BlockSpec((tm, d), lambda i: (i, 0)),
                  pl.BlockSpec((1, d), lambda i: (0, 0)),
                  pl.BlockSpec((d, 3 * d), lambda i: (0, 0)),
                  pl.BlockSpec((tm, LANES), lambda i: (i, 0)),
                  pl.BlockSpec((tm, LANES), lambda i: (i, 0))],
        out_specs=pl.BlockSpec((1, dil, lt, 3 * d), lambda i: (i // nst, 0, i % nst, 0)),
        out_shape=jax.ShapeDtypeStruct((batch, dil, seq // dil, 3 * d), BF16),
        scratch_shapes=stage,
        compiler_params=_params("parallel"),
        name=f"qkv_dilated_d{dil}",
    )(x2, g, w_bf, cos, sin)


def _band_attn_kernel(q_ref, km_ref, kl_ref, kr_ref, vm_ref, vl_ref, vr_ref, o_ref, lse_ref,
                      kc_ref, vc_ref, *, lt, sub_len):
    i = pl.program_id(2)
    r = DB_RADIUS

    @pl.when((pl.program_id(0) == 0) & (pl.program_id(1) == 0) & (i == 0))
    def _():
        vc_ref[...] = jnp.ones(vc_ref.shape, BF16)

    kc_ref[0:r] = kl_ref[0, 0]
    kc_ref[r:r + lt] = km_ref[0, 0]
    kc_ref[r + lt:r + lt + r] = kr_ref[0, 0]
    for hh in range(DB_HEADS):
        sl = slice(hh * HEAD_W, (hh + 1) * HEAD_W)
        dst = slice(2 * hh * HEAD_W, (2 * hh + 1) * HEAD_W)
        vc_ref[0:r, dst] = vl_ref[0, 0, :, sl]
        vc_ref[r:r + lt, dst] = vm_ref[0, 0, :, sl]
        vc_ref[r + lt:r + lt + r, dst] = vr_ref[0, 0, :, sl]

    qb = BAND_QB
    kb = qb + 2 * r
    qi = lax.broadcasted_iota(jnp.int32, (qb, kb), 0)
    ki = lax.broadcasted_iota(jnp.int32, (qb, kb), 1)
    band = jnp.abs(ki - r - qi) <= r
    lane = lax.broadcasted_iota(jnp.int32, (qb, LANES), 1)

    for j in range(lt // qb):
        q0 = j * qb
        key_pos = i * lt + (q0 - r) + ki
        mask = band & (key_pos >= 0) & (key_pos < sub_len)
        lse_all = jnp.zeros((qb, LANES), F32)
        for hh in range(DB_HEADS):
            sl = slice(hh * HEAD_W, (hh + 1) * HEAD_W)
            s = _dot_nt(q_ref[0, 0, q0:q0 + qb, sl], kc_ref[q0:q0 + kb, sl])
            s = jnp.where(mask, s, NEG_BIG)
            m = jnp.max(s, axis=-1, keepdims=True)
            p = jnp.exp(s - m).astype(BF16)
            oa = _dot(p, vc_ref[q0:q0 + kb, 2 * hh * HEAD_W:(2 * hh + 2) * HEAD_W])
            den = oa[:, HEAD_W:]
            o_ref[0, 0, q0:q0 + qb, sl] = (oa[:, :HEAD_W] * (1.0 / den)).astype(BF16)
            lse_all = jnp.where(lane == hh, m + jnp.log(den), lse_all)
        lse_ref[0, 0, q0:q0 + qb, :] = lse_all


def _band_attention(qkv4):
    batch, dil, sub_len, n = qkv4.shape
    d = n // 3
    lt = min(BAND_LT, sub_len)
    r = DB_RADIUS
    hpt = lt // r
    n_halo = sub_len // r

    def main_spec(which):
        return pl.BlockSpec((1, 1, lt, d), lambda b, rr, i: (b, rr, i, which))

    def left_spec(which):
        return pl.BlockSpec((1, 1, r, d), lambda b, rr, i: (
            b, rr, jnp.maximum(i * hpt - 1, 0), which))

    def right_spec(which):
        return pl.BlockSpec((1, 1, r, d), lambda b, rr, i: (
            b, rr, jnp.minimum((i + 1) * hpt, n_halo - 1), which))

    return pl.pallas_call(
        functools.partial(_band_attn_kernel, lt=lt, sub_len=sub_len),
        grid=(batch, dil, sub_len // lt),
        in_specs=[main_spec(0), main_spec(1), left_spec(1), right_spec(1),
                  main_spec(2), left_spec(2), right_spec(2)],
        out_specs=[pl.BlockSpec((1, 1, lt, d), lambda b, rr, i: (b, rr, i, 0)),
                   pl.BlockSpec((1, 1, lt, LANES), lambda b, rr, i: (b, rr, i, 0))],
        out_shape=[jax.ShapeDtypeStruct((batch, dil, sub_len, d), BF16),
                   jax.ShapeDtypeStruct((batch, dil, sub_len, LANES), F32)],
        scratch_shapes=[pltpu.VMEM((lt + 2 * r, d), BF16),
                        pltpu.VMEM((lt + 2 * r, 2 * d), BF16)],
        compiler_params=_params("arbitrary", "arbitrary", "arbitrary"),
        name=f"band_attention_d{dil}",
    )(qkv4, qkv4, qkv4, qkv4, qkv4, qkv4, qkv4)


def _combine_tail_kernel(o0_ref, o1_ref, o2_ref, l0_ref, l1_ref, l2_ref, w_ref, g_ref, x_ref,
                         *rest, dils):
    tm = x_ref.shape[0]
    n_slab = x_ref.shape[1] // HEAD_W
    o_refs = (o0_ref, o1_ref, o2_ref)
    l_refs = (l0_ref, l1_ref, l2_ref)
    stage = list(rest[N_XA_REFS + N_FF_REFS + 1:])
    o_tok, l_tok = [], []
    for gi, dil in enumerate(dils):
        if dil == 1:
            o_tok.append(None)
            l_tok.append(l_refs[gi][0, 0])
            continue
        so_ref = stage.pop(0)
        sl_ref = stage.pop(0)
        for r in range(dil):
            rows = pl.ds(r, tm // dil, stride=dil)
            sl_ref[rows, :] = l_refs[gi][0, r]
            for hh in range(n_slab):
                so_ref[hh, rows, :] = o_refs[gi][0, r, :, hh * HEAD_W:(hh + 1) * HEAD_W].astype(F32)
        o_tok.append(so_ref)
        l_tok.append(sl_ref[...])
    mx = jnp.maximum(jnp.maximum(l_tok[0], l_tok[1]), l_tok[2])
    es = [jnp.exp(l - mx) for l in l_tok]
    inv = 1.0 / (es[0] + es[1] + es[2])
    ws = [e * inv for e in es]
    outs = []
    for hh in range(n_slab):
        sl = slice(hh * HEAD_W, (hh + 1) * HEAD_W)
        c = slice(hh, hh + 1)
        oh = None
        for gi, dil in enumerate(dils):
            og = o_refs[gi][0, 0, :, sl].astype(F32) if dil == 1 else o_tok[gi][hh]
            term = ws[gi][:, c] * og
            oh = term if oh is None else oh + term
        outs.append(oh.astype(BF16))
    _finish_layer(jnp.concatenate(outs, axis=1), w_ref, g_ref, x_ref, rest)


def _combine_layer_tail(os, lses, w_o, mix_g, x2, xa, ff, seq, mem_len):
    t, d = x2.shape
    tm = min(ROW_TILE, seq)
    nst = seq // tm
    dils = tuple(o.shape[1] for o in os)

    def group_spec(dil, width):
        return pl.BlockSpec((1, dil, tm // dil, width), lambda i: (i // nst, 0, i % nst, 0))

    stage = []
    for dil in dils:
        if dil > 1:
            stage += [pltpu.VMEM((d // HEAD_W, tm, HEAD_W), F32), pltpu.VMEM((tm, LANES), F32)]
    in_specs, row = _tail_specs(tm, d, ff[1].shape[1], mem_len, nst)
    pre_g, wq, kv, wo, post_g = xa
    return pl.pallas_call(
        functools.partial(_combine_tail_kernel, dils=dils),
        grid=(t // tm,),
        in_specs=[group_spec(dil, d) for dil in dils] + [group_spec(dil, LANES) for dil in dils]
                 + in_specs,
        out_specs=row,
        out_shape=jax.ShapeDtypeStruct((t, d), F32),
        scratch_shapes=stage,
        compiler_params=pltpu.CompilerParams(dimension_semantics=("parallel",),
                                             vmem_limit_bytes=TAIL_VMEM_LIMIT),
        name="combine_layer_tail",
    )(*os, *lses, w_o, mix_g, x2, pre_g, wq, kv, kv, wo, post_g, *ff)


def _row(v):
    return v.reshape(1, -1)


def kernel(x, mem, positions, mix_pre_g, mix_post_g, da_w_qkv, da_lambda, da_subln_g, da_w_o,
           db_w_qkv, db_w_o, xa_pre_g, mem_g, xa_wq, xa_wkv, xa_wo, xa_post_g,
           ffn_pre_g, w_up, w_down, ffn_post_g):
    batch, seq, d = x.shape
    mem_len = mem.shape[1]
    depth = mix_pre_g.shape[0]
    n_mixers = 2
    x2 = x.reshape(batch * seq, d)
    mem2 = mem.reshape(batch * mem_len, d)
    pos_col = positions.astype(F32).reshape(batch * seq, 1)

    for i in range(depth):
        j = i // n_mixers
        kv = _mem_kv(mem2, _row(mem_g[i]), xa_wkv[i].astype(BF16))
        xa = (_row(xa_pre_g[i]), xa_wq[i].astype(BF16), kv, xa_wo[i].astype(BF16),
              _row(xa_post_g[i]))
        ff = (_row(ffn_pre_g[i]), w_up[i].astype(BF16), w_down[i].astype(BF16),
              _row(ffn_post_g[i]))
        if i % n_mixers == 0:
            lambda_init = 0.8 - 0.6 * math.exp(-0.3 * i)
            cos, sin = _rope_tables(pos_col, DA_QK_DIM)
            qt, k2, vt = _qkv_diff(x2, _row(mix_pre_g[i]), da_w_qkv[j].astype(BF16), cos, sin,
                                   batch, seq)
            o = _diff_attention(qt, k2.reshape(batch, seq, d), vt, da_lambda[j],
                                _row(da_subln_g[j]), lambda_init)
            x2 = _layer_tail(o.reshape(batch * seq, d), da_w_o[j].astype(BF16),
                             _row(mix_post_g[i]), x2, xa, ff, seq, mem_len)
        else:
            cos, sin = _rope_tables(pos_col, HEAD_W)
            groups = len(DB_GROUPS)
            w4 = db_w_qkv[j].astype(BF16).reshape(d, 3, groups, d)
            os, lses = [], []
            for gi, (_, dil) in enumerate(DB_GROUPS):
                qkv4 = _qkv_dilated(x2, _row(mix_pre_g[i]), w4[:, :, gi, :].reshape(d, 3 * d),
                                    cos, sin, batch, seq, dil)
                o, lse = _band_attention(qkv4)
                os.append(o)
                lses.append(lse)
            x2 = _combine_layer_tail(os, lses, db_w_o[j].astype(BF16), _row(mix_post_g[i]), x2,
                                     xa, ff, seq, mem_len)
    return x2.reshape(batch, seq, d)
```

```python
import functools
import math

import jax
import jax.numpy as jnp
from jax import lax
from jax.experimental import pallas as pl
from jax.experimental.pallas import tpu as pltpu

F32 = jnp.float32
BF16 = jnp.bfloat16

NORM_EPS = 1e-6
ROPE_THETA = 10000.0
LANES = 128
HEAD_W = 128
DA_HEADS = 8
DA_QK_DIM = 64
DB_GROUPS = ((128, 1), (512, 4), (2048, 16))
DB_HEADS = 8
DB_RADIUS = 64
XA_HEADS = 4
XA_HEAD_DIM = 256
NEG_BIG = -1e30
VMEM_LIMIT = 48 * 1024 * 1024
TAIL_VMEM_LIMIT = 56 * 1024 * 1024
MLP_CHUNK = 1024
TAIL_SPLIT = 2

ROW_TILE = 512
ATT_TQ = 2048
ATT_TK = 512
BAND_LT = 512
BAND_QB = 128


def _params(*sem):
    return pltpu.CompilerParams(dimension_semantics=sem, vmem_limit_bytes=VMEM_LIMIT)


def _dot(a, b):
    return jnp.dot(a, b, preferred_element_type=F32)


def _dot_nt(a, b):
    return lax.dot_general(a, b, (((1,), (1,)), ((), ())), preferred_element_type=F32)


def _rms(x, g):
    ms = jnp.mean(x * x, axis=-1, keepdims=True)
    return x * lax.rsqrt(ms + NORM_EPS) * g


def _rope_slab(t, cos, sin, half):
    if half == 64:
        rot = pltpu.roll(t, 64, 1)
    else:
        lane = lax.broadcasted_iota(jnp.int32, t.shape, 1)
        rot = jnp.where((lane & 32) == 0, pltpu.roll(t, 96, 1), pltpu.roll(t, 32, 1))
    return t * cos + rot * sin


def _rope_table_kernel(pos_ref, inv_ref, sign_ref, cos_ref, sin_ref):
    ang = pos_ref[...] * inv_ref[...]
    cos_ref[...] = jnp.cos(ang)
    sin_ref[...] = jnp.sin(ang) * sign_ref[...]


def _rope_tables(pos_col, dh):
    t = pos_col.shape[0]
    inv = ROPE_THETA ** (-jnp.arange(0, dh, 2, dtype=F32) / dh)
    reps = LANES // dh
    inv_full = jnp.tile(jnp.concatenate([inv, inv]), reps).reshape(1, LANES)
    sign = jnp.tile(jnp.concatenate([-jnp.ones(dh // 2, F32), jnp.ones(dh // 2, F32)]),
                    reps).reshape(1, LANES)
    tm = min(t, 2048)
    return pl.pallas_call(
        _rope_table_kernel,
        grid=(t // tm,),
        in_specs=[pl.BlockSpec((tm, 1), lambda i: (i, 0)),
                  pl.BlockSpec((1, LANES), lambda i: (0, 0)),
                  pl.BlockSpec((1, LANES), lambda i: (0, 0))],
        out_specs=[pl.BlockSpec((tm, LANES), lambda i: (i, 0)),
                   pl.BlockSpec((tm, LANES), lambda i: (i, 0))],
        out_shape=[jax.ShapeDtypeStruct((t, LANES), F32)] * 2,
        compiler_params=_params("parallel"),
        name="rope_tables",
    )(pos_col, inv_full, sign)


def _qkv_diff_kernel(x_ref, g_ref, w_ref, cos_ref, sin_ref, qt_ref, k_ref, vt_ref, *, scale):
    d = x_ref.shape[1]
    h = _rms(x_ref[...], g_ref[...]).astype(BF16)
    cos = cos_ref[...]
    sin = sin_ref[...]
    q = _dot(h, w_ref[:, 0:d])
    for hh in range(DA_HEADS):
        sl = slice(hh * HEAD_W, (hh + 1) * HEAD_W)
        t = _rope_slab(q[:, sl], cos, sin, DA_QK_DIM // 2) * scale
        qt_ref[0, 0, sl, :] = t.T.astype(BF16)
    k = _dot(h, w_ref[:, d:2 * d])
    for hh in range(DA_HEADS):
        sl = slice(hh * HEAD_W, (hh + 1) * HEAD_W)
        k_ref[:, sl] = _rope_slab(k[:, sl], cos, sin, DA_QK_DIM // 2).astype(BF16)
    v = _dot(h, w_ref[:, 2 * d:3 * d])
    for hh in range(DA_HEADS):
        sl = slice(hh * HEAD_W, (hh + 1) * HEAD_W)
        vt_ref[0, 0, sl, :] = v[:, sl].T.astype(BF16)


def _qkv_diff(x2, g, w_bf, cos, sin, batch, seq):
    t, d = x2.shape
    tm = min(ATT_TK, seq)
    nst = seq // tm
    scale = DA_QK_DIM ** -0.5 * math.log2(math.e)
    vt_rows = d
    return pl.pallas_call(
        functools.partial(_qkv_diff_kernel, scale=scale),
        grid=(t // tm,),
        in_specs=[pl.BlockSpec((tm, d), lambda i: (i, 0)),
                  pl.BlockSpec((1, d), lambda i: (0, 0)),
                  pl.BlockSpec((d, 3 * d), lambda i: (0, 0)),
                  pl.BlockSpec((tm, LANES), lambda i: (i, 0)),
                  pl.BlockSpec((tm, LANES), lambda i: (i, 0))],
        out_specs=[pl.BlockSpec((1, 1, d, tm), lambda i: (i // nst, i % nst, 0, 0)),
                   pl.BlockSpec((tm, d), lambda i: (i, 0)),
                   pl.BlockSpec((1, 1, vt_rows, tm), lambda i: (i // nst, i % nst, 0, 0))],
        out_shape=[jax.ShapeDtypeStruct((batch, nst, d, tm), BF16),
                   jax.ShapeDtypeStruct((t, d), BF16),
                   jax.ShapeDtypeStruct((batch, nst, vt_rows, tm), BF16)],
        compiler_params=_params("parallel"),
        name="qkv_diff",
    )(x2, g, w_bf, cos, sin)


def _diff_attn_kernel(qt_ref, k_ref, vt_ref, lam_ref, g_ref, o_ref,
                      qp_ref, s0_ref, s1_ref, e0_ref, e1_ref, t0_ref, t1_ref, a0_ref, a1_ref,
                      m_ref, l_ref, acc_ref, *, nk, tk, lambda_init):
    qt = jnp.concatenate([qt_ref[0, j] for j in range(qt_ref.shape[1])], axis=1)
    row = lax.broadcasted_iota(jnp.int32, qt.shape, 0)
    zero = jnp.zeros_like(qt)
    qp_ref[0] = jnp.where(row < DA_QK_DIM, qt, zero)
    qp_ref[1] = jnp.where(row >= DA_QK_DIM, qt, zero)
    m_ref[...] = jnp.full(m_ref.shape, NEG_BIG, F32)
    l_ref[...] = jnp.zeros(l_ref.shape, F32)
    acc_ref[...] = jnp.zeros(acc_ref.shape, F32)
    e1_ref[...] = jnp.zeros(e1_ref.shape, BF16)
    a1_ref[...] = jnp.zeros(a1_ref.shape, F32)

    def scores(t, s_ref, tmax_ref):
        kb = k_ref[0, pl.ds(pl.multiple_of(t * tk, tk), tk), :]
        for c in range(2):
            s = _dot(kb, qp_ref[c])
            s_ref[c] = s
            tmax_ref[c] = jnp.max(s, axis=0, keepdims=True)

    def exponentiate(s_ref, tmax_ref, e_ref, alpha_ref):
        for c in range(2):
            m_prev = m_ref[c]
            m_new = jnp.maximum(m_prev, tmax_ref[c])
            alpha_ref[c] = jnp.exp2(m_prev - m_new)
            m_ref[c] = m_new
            e = jnp.exp2(s_ref[c] - m_new)
            l_ref[c] = alpha_ref[c] * l_ref[c] + jnp.sum(e, axis=0, keepdims=True)
            e_ref[c] = e.astype(BF16)

    def accumulate(t, e_ref, alpha_ref):
        vt = vt_ref[0, t]
        for c in range(2):
            acc_ref[c] = alpha_ref[c] * acc_ref[c] + _dot(vt, e_ref[c])

    scores(0, s0_ref, t0_ref)

    def kv_pair(u, carry):
        t = 2 * u
        scores(t + 1, s1_ref, t1_ref)
        exponentiate(s0_ref, t0_ref, e0_ref, a0_ref)
        accumulate(jnp.maximum(t - 1, 0), e1_ref, a1_ref)
        scores(jnp.minimum(t + 2, nk - 1), s0_ref, t0_ref)
        exponentiate(s1_ref, t1_ref, e1_ref, a1_ref)
        accumulate(t, e0_ref, a0_ref)
        return carry

    lax.fori_loop(0, nk // 2, kv_pair, 0)
    accumulate(nk - 1, e1_ref, a1_ref)

    lam = lam_ref[...]
    lam_full = (jnp.exp(jnp.sum(lam[0:1] * lam[1:2], axis=-1, keepdims=True))
                - jnp.exp(jnp.sum(lam[2:3] * lam[3:4], axis=-1, keepdims=True))
                + lambda_init)
    ot = acc_ref[0] * (1.0 / l_ref[0]) - lam_full * (acc_ref[1] * (1.0 / l_ref[1]))
    o = _rms(ot.T, g_ref[...]) * (1.0 - lambda_init)
    o_ref[0] = o.astype(BF16)


def _diff_attention(qt, k3, vt, lam, subln_g, lambda_init):
    batch, nk, d, tk = qt.shape
    seq = k3.shape[1]
    tq = min(ATT_TQ, seq)
    nqt = tq // tk
    nq = seq // tq
    assert nk % 2 == 0 and tq % tk == 0
    return pl.pallas_call(
        functools.partial(_diff_attn_kernel, nk=nk, tk=tk, lambda_init=lambda_init),
        grid=(batch, DA_HEADS, nq),
        in_specs=[pl.BlockSpec((1, nqt, HEAD_W, tk), lambda b, h, i: (b, i, h, 0)),
                  pl.BlockSpec((1, seq, HEAD_W), lambda b, h, i: (b, 0, h)),
                  pl.BlockSpec((1, nk, HEAD_W, tk), lambda b, h, i: (b, 0, h, 0)),
                  pl.BlockSpec(lam.shape, lambda b, h, i: (0, 0)),
                  pl.BlockSpec((1, HEAD_W), lambda b, h, i: (0, 0))],
        out_specs=pl.BlockSpec((1, tq, HEAD_W), lambda b, h, i: (b, i, h)),
        out_shape=jax.ShapeDtypeStruct((batch, seq, d), BF16),
        scratch_shapes=[pltpu.VMEM((2, HEAD_W, tq), BF16),
                        pltpu.VMEM((2, tk, tq), F32),
                        pltpu.VMEM((2, tk, tq), F32),
                        pltpu.VMEM((2, tk, tq), BF16),
                        pltpu.VMEM((2, tk, tq), BF16),
                        pltpu.VMEM((2, 1, tq), F32),
                        pltpu.VMEM((2, 1, tq), F32),
                        pltpu.VMEM((2, 1, tq), F32),
                        pltpu.VMEM((2, 1, tq), F32),
                        pltpu.VMEM((2, 1, tq), F32),
                        pltpu.VMEM((2, 1, tq), F32),
                        pltpu.VMEM((2, HEAD_W, tq), F32)],
        compiler_params=_params("parallel", "parallel", "parallel"),
        name="diff_attention",
    )(qt, k3, vt, lam, subln_g)


def _mem_kv_kernel(mem_ref, g_ref, w_ref, o_ref):
    mn = _rms(mem_ref[...], g_ref[...]).astype(BF16)
    o_ref[...] = _dot(mn, w_ref[...]).astype(BF16)


def _mem_kv(mem2, g, w_bf):
    rows, d = mem2.shape
    n = w_bf.shape[1]
    tn = n // 2
    return pl.pallas_call(
        _mem_kv_kernel,
        grid=(n // tn,),
        in_specs=[pl.BlockSpec((rows, d), lambda j: (0, 0)),
                  pl.BlockSpec((1, d), lambda j: (0, 0)),
                  pl.BlockSpec((d, tn), lambda j: (0, j))],
        out_specs=pl.BlockSpec((rows, tn), lambda j: (0, j)),
        out_shape=jax.ShapeDtypeStruct((rows, n), BF16),
        compiler_params=_params("parallel"),
        name="mem_kv",
    )(mem2, g, w_bf)


def _cross_attn_heads(q, k_ref, v_ref):
    outs = []
    for hh in range(XA_HEADS):
        sl = slice(hh * XA_HEAD_DIM, (hh + 1) * XA_HEAD_DIM)
        s = _dot_nt(q[:, sl], k_ref[:, sl])
        e = jnp.exp(s - jnp.max(s, axis=-1, keepdims=True))
        p = e * (1.0 / jnp.sum(e, axis=-1, keepdims=True))
        outs.append(_dot(p.astype(BF16), v_ref[:, sl]).astype(BF16))
    return jnp.concatenate(outs, axis=1)


N_XA_REFS = 6
N_FF_REFS = 4


def _finish_layer(a, w_ref, g_ref, x_ref, rest):
    pre_g_ref, wq_ref, k_ref, v_ref, wo_ref, post_g_ref = rest[:N_XA_REFS]
    fpre_ref, wu_ref, wd_ref, fpost_ref = rest[N_XA_REFS:N_XA_REFS + N_FF_REFS]
    out_ref = rest[N_XA_REFS + N_FF_REFS]
    hs = x_ref.shape[0] // TAIL_SPLIT
    rows = [slice(p * hs, (p + 1) * hs) for p in range(TAIL_SPLIT)]
    parts = range(TAIL_SPLIT)
    y = [_dot(a[rows[p]], w_ref[...]) for p in parts]
    x = [x_ref[rows[p], :] + _rms(y[p], g_ref[...]) for p in parts]
    h = [_rms(x[p], pre_g_ref[...]).astype(BF16) for p in parts]
    q = [(_dot(h[p], wq_ref[...]) * (XA_HEAD_DIM ** -0.5)).astype(BF16) for p in parts]
    o = [_cross_attn_heads(q[p], k_ref, v_ref) for p in parts]
    y = [_dot(o[p], wo_ref[...]) for p in parts]
    x = [x[p] + _rms(y[p], post_g_ref[...]) for p in parts]
    h = [_rms(x[p], fpre_ref[...]).astype(BF16) for p in parts]
    d_ff = wu_ref.shape[1]
    chunk = min(MLP_CHUNK, d_ff)
    acc = [jnp.zeros(x[p].shape, F32) for p in parts]
    for c in range(d_ff // chunk):
        sl = slice(c * chunk, (c + 1) * chunk)
        for p in parts:
            u = jnp.maximum(_dot(h[p], wu_ref[:, sl]), 0.0)
            acc[p] = acc[p] + _dot((u * u).astype(BF16), wd_ref[sl, :])
    for p in parts:
        out_ref[rows[p], :] = x[p] + _rms(acc[p], fpost_ref[...])


def _tail_kernel(a_ref, w_ref, g_ref, x_ref, *rest):
    _finish_layer(a_ref[...], w_ref, g_ref, x_ref, rest)


def _tail_specs(tm, d, d_ff, mem_len, nst):
    def const(*shape):
        return pl.BlockSpec(shape, lambda i: (0,) * len(shape), pipeline_mode=pl.Buffered(1))

    row = pl.BlockSpec((tm, d), lambda i: (i, 0))
    in_specs = [const(d, d), const(1, d), row,
                const(1, d), const(d, d),
                pl.BlockSpec((mem_len, d), lambda i: (i // nst, 0)),
                pl.BlockSpec((mem_len, d), lambda i: (i // nst, 1)),
                const(d, d), const(1, d),
                const(1, d), const(d, d_ff), const(d_ff, d), const(1, d)]
    return in_specs, row


def _layer_tail(a2, w_o, mix_g, x2, xa, ff, seq, mem_len):
    t, d = x2.shape
    tm = min(ROW_TILE, seq)
    in_specs, row = _tail_specs(tm, d, ff[1].shape[1], mem_len, seq // tm)
    pre_g, wq, kv, wo, post_g = xa
    return pl.pallas_call(
        _tail_kernel,
        grid=(t // tm,),
        in_specs=[row] + in_specs,
        out_specs=row,
        out_shape=jax.ShapeDtypeStruct((t, d), F32),
        compiler_params=pltpu.CompilerParams(dimension_semantics=("parallel",),
                                             vmem_limit_bytes=TAIL_VMEM_LIMIT),
        name="layer_tail",
    )(a2, w_o, mix_g, x2, pre_g, wq, kv, kv, wo, post_g, *ff)


def _qkv_dil_kernel(x_ref, g_ref, wq_ref, wk_ref, wv_ref, cos_ref, sin_ref, o_ref, *stage,
                    scale, dil):
    tm, d = x_ref.shape
    lt = tm // dil
    n_slab = d // HEAD_W
    hn = _rms(x_ref[...], g_ref[...])
    if dil == 1:
        h = hn.astype(BF16)
        cos = cos_ref[...]
        sin = sin_ref[...]
    else:
        f_ref, hp_ref, tp_ref = stage
        for hh in range(n_slab):
            f_ref[hh] = hn[:, hh * HEAD_W:(hh + 1) * HEAD_W]
        f_ref[n_slab] = cos_ref[...]
        f_ref[n_slab + 1] = sin_ref[...]
        for r in range(dil):
            src = pl.ds(r, lt, stride=dil)
            dst = slice(r * lt, (r + 1) * lt)
            for hh in range(n_slab):
                hp_ref[dst, hh * HEAD_W:(hh + 1) * HEAD_W] = f_ref[hh, src, :].astype(BF16)
            tp_ref[0, dst, :] = f_ref[n_slab, src, :]
            tp_ref[1, dst, :] = f_ref[n_slab + 1, src, :]
        h = hp_ref[...]
        cos = tp_ref[0]
        sin = tp_ref[1]
    for which in range(3):
        y = _dot(h, (wq_ref, wk_ref, wv_ref)[which][...])
        for hh in range(n_slab):
            col = slice(which * d + hh * HEAD_W, which * d + (hh + 1) * HEAD_W)
            t = y[:, hh * HEAD_W:(hh + 1) * HEAD_W]
            if which < 2:
                t = _rope_slab(t, cos, sin, HEAD_W // 2)
            if which == 0:
                t = t * scale
            t = t.astype(BF16)
            for r in range(dil):
                o_ref[0, r, :, col] = t[r * lt:(r + 1) * lt]


def _qkv_dilated(x2, g, w_bf, cos, sin, batch, seq, group, dil):
    groups = len(DB_GROUPS)
    t, d = x2.shape
    tm = min(ROW_TILE, seq)
    nst = seq // tm
    lt = tm // dil
    stage = []
    if dil > 1:
        stage = [pltpu.VMEM((d // HEAD_W + 2, tm, HEAD_W), F32),
                 pltpu.VMEM((tm, d), BF16),
                 pltpu.VMEM((2, tm, HEAD_W), F32)]
    return pl.pallas_call(
        functools.partial(_qkv_dil_kernel, scale=HEAD_W ** -0.5, dil=dil),
        grid=(t // tm,),
        in_specs=[pl.BlockSpec((tm, d), lambda i: (i, 0)),
                  pl.BlockSpec((1, d), lambda i: (0, 0)),
                  pl.BlockSpec((d, d), lambda i: (0, group)),
                  pl.BlockSpec((d, d), lambda i: (0, groups + group)),
                  pl.BlockSpec((d, d), lambda i: (0, 2 * groups + group)),
                  pl.BlockSpec((tm, LANES), lambda i: (i, 0)),
                  pl.BlockSpec((tm, LANES), lambda i: (i, 0))],
        out_specs=pl.BlockSpec((1, dil, lt, 3 * d), lambda i: (i // nst, 0, i % nst, 0)),
        out_shape=jax.ShapeDtypeStruct((batch, dil, seq // dil, 3 * d), BF16),
        scratch_shapes=stage,
        compiler_params=_params("parallel"),
        name=f"qkv_dilated_d{dil}",
    )(x2, g, w_bf, w_bf, w_bf, cos, sin)


def _band_attn_kernel(q_ref, km_ref, kl_ref, kr_ref, vm_ref, vl_ref, vr_ref, o_ref, lse_ref,
                      kc_ref, vc_ref, *, lt, sub_len):
    i = pl.program_id(2)
    r = DB_RADIUS

    @pl.when((pl.program_id(0) == 0) & (pl.program_id(1) == 0) & (i == 0))
    def _():
        vc_ref[...] = jnp.ones(vc_ref.shape, BF16)

    kc_ref[0:r] = kl_ref[0, 0]
    kc_ref[r:r + lt] = km_ref[0, 0]
    kc_ref[r + lt:r + lt + r] = kr_ref[0, 0]
    for hh in range(DB_HEADS):
        sl = slice(hh * HEAD_W, (hh + 1) * HEAD_W)
        dst = slice(2 * hh * HEAD_W, (2 * hh + 1) * HEAD_W)
        vc_ref[0:r, dst] = vl_ref[0, 0, :, sl]
        vc_ref[r:r + lt, dst] = vm_ref[0, 0, :, sl]
        vc_ref[r + lt:r + lt + r, dst] = vr_ref[0, 0, :, sl]

    qb = BAND_QB
    kb = qb + 2 * r
    qi = lax.broadcasted_iota(jnp.int32, (qb, kb), 0)
    ki = lax.broadcasted_iota(jnp.int32, (qb, kb), 1)
    band = jnp.abs(ki - r - qi) <= r
    lane = lax.broadcasted_iota(jnp.int32, (qb, LANES), 1)

    for j in range(lt // qb):
        q0 = j * qb
        key_pos = i * lt + (q0 - r) + ki
        mask = band & (key_pos >= 0) & (key_pos < sub_len)
        lse_all = jnp.zeros((qb, LANES), F32)
        for hh in range(DB_HEADS):
            sl = slice(hh * HEAD_W, (hh + 1) * HEAD_W)
            s = _dot_nt(q_ref[0, 0, q0:q0 + qb, sl], kc_ref[q0:q0 + kb, sl])
            s = jnp.where(mask, s, NEG_BIG)
            m = jnp.max(s, axis=-1, keepdims=True)
            p = jnp.exp(s - m).astype(BF16)
            oa = _dot(p, vc_ref[q0:q0 + kb, 2 * hh * HEAD_W:(2 * hh + 2) * HEAD_W])
            den = oa[:, HEAD_W:]
            o_ref[0, 0, q0:q0 + qb, sl] = (oa[:, :HEAD_W] * (1.0 / den)).astype(BF16)
            lse_all = jnp.where(lane == hh, m + jnp.log(den), lse_all)
        lse_ref[0, 0, q0:q0 + qb, :] = lse_all


def _band_attention(qkv4):
    batch, dil, sub_len, n = qkv4.shape
    d = n // 3
    lt = min(BAND_LT, sub_len)
    r = DB_RADIUS
    hpt = lt // r
    n_halo = sub_len // r

    def main_spec(which):
        return pl.BlockSpec((1, 1, lt, d), lambda b, rr, i: (b, rr, i, which))

    def left_spec(which):
        return pl.BlockSpec((1, 1, r, d), lambda b, rr, i: (
            b, rr, jnp.maximum(i * hpt - 1, 0), which))

    def right_spec(which):
        return pl.BlockSpec((1, 1, r, d), lambda b, rr, i: (
            b, rr, jnp.minimum((i + 1) * hpt, n_halo - 1), which))

    return pl.pallas_call(
        functools.partial(_band_attn_kernel, lt=lt, sub_len=sub_len),
        grid=(batch, dil, sub_len // lt),
        in_specs=[main_spec(0), main_spec(1), left_spec(1), right_spec(1),
                  main_spec(2), left_spec(2), right_spec(2)],
        out_specs=[pl.BlockSpec((1, 1, lt, d), lambda b, rr, i: (b, rr, i, 0)),
                   pl.BlockSpec((1, 1, lt, LANES), lambda b, rr, i: (b, rr, i, 0))],
        out_shape=[jax.ShapeDtypeStruct((batch, dil, sub_len, d), BF16),
                   jax.ShapeDtypeStruct((batch, dil, sub_len, LANES), F32)],
        scratch_shapes=[pltpu.VMEM((lt + 2 * r, d), BF16),
                        pltpu.VMEM((lt + 2 * r, 2 * d), BF16)],
        compiler_params=_params("arbitrary", "arbitrary", "arbitrary"),
        name=f"band_attention_d{dil}",
    )(qkv4, qkv4, qkv4, qkv4, qkv4, qkv4, qkv4)


def _combine_tail_kernel(o0_ref, o1_ref, o2_ref, l0_ref, l1_ref, l2_ref, w_ref, g_ref, x_ref,
                         *rest, dils):
    tm = x_ref.shape[0]
    n_slab = x_ref.shape[1] // HEAD_W
    o_refs = (o0_ref, o1_ref, o2_ref)
    l_refs = (l0_ref, l1_ref, l2_ref)
    stage = list(rest[N_XA_REFS + N_FF_REFS + 1:])
    o_tok, l_tok = [], []
    for gi, dil in enumerate(dils):
        if dil == 1:
            o_tok.append(None)
            l_tok.append(l_refs[gi][0, 0])
            continue
        so_ref = stage.pop(0)
        sl_ref = stage.pop(0)
        for r in range(dil):
            rows = pl.ds(r, tm // dil, stride=dil)
            sl_ref[rows, :] = l_refs[gi][0, r]
            for hh in range(n_slab):
                so_ref[hh, rows, :] = o_refs[gi][0, r, :, hh * HEAD_W:(hh + 1) * HEAD_W].astype(F32)
        o_tok.append(so_ref)
        l_tok.append(sl_ref[...])
    mx = jnp.maximum(jnp.maximum(l_tok[0], l_tok[1]), l_tok[2])
    es = [jnp.exp(l - mx) for l in l_tok]
    inv = 1.0 / (es[0] + es[1] + es[2])
    ws = [e * inv for e in es]
    outs = []
    for hh in range(n_slab):
        sl = slice(hh * HEAD_W, (hh + 1) * HEAD_W)
        c = slice(hh, hh + 1)
        oh = None
        for gi, dil in enumerate(dils):
            og = o_refs[gi][0, 0, :, sl].astype(F32) if dil == 1 else o_tok[gi][hh]
            term = ws[gi][:, c] * og
            oh = term if oh is None else oh + term
        outs.append(oh.astype(BF16))
    _finish_layer(jnp.concatenate(outs, axis=1), w_ref, g_ref, x_ref, rest)


def _combine_layer_tail(os, lses, w_o, mix_g, x2, xa, ff, seq, mem_len):
    t, d = x2.shape
    tm = min(ROW_TILE, seq)
    nst = seq // tm
    dils = tuple(o.shape[1] for o in os)

    def group_spec(dil, width):
        return pl.BlockSpec((1, dil, tm // dil, width), lambda i: (i // nst, 0, i % nst, 0))

    stage = []
    for dil in dils:
        if dil > 1:
            stage += [pltpu.VMEM((d // HEAD_W, tm, HEAD_W), F32), pltpu.VMEM((tm, LANES), F32)]
    in_specs, row = _tail_specs(tm, d, ff[1].shape[1], mem_len, nst)
    pre_g, wq, kv, wo, post_g = xa
    return pl.pallas_call(
        functools.partial(_combine_tail_kernel, dils=dils),
        grid=(t // tm,),
        in_specs=[group_spec(dil, d) for dil in dils] + [group_spec(dil, LANES) for dil in dils]
                 + in_specs,
        out_specs=row,
        out_shape=jax.ShapeDtypeStruct((t, d), F32),
        scratch_shapes=stage,
        compiler_params=pltpu.CompilerParams(dimension_semantics=("parallel",),
                                             vmem_limit_bytes=TAIL_VMEM_LIMIT),
        name="combine_layer_tail",
    )(*os, *lses, w_o, mix_g, x2, pre_g, wq, kv, kv, wo, post_g, *ff)


def _row(v):
    return v.reshape(1, -1)


def kernel(x, mem, positions, mix_pre_g, mix_post_g, da_w_qkv, da_lambda, da_subln_g, da_w_o,
           db_w_qkv, db_w_o, xa_pre_g, mem_g, xa_wq, xa_wkv, xa_wo, xa_post_g,
           ffn_pre_g, w_up, w_down, ffn_post_g):
    batch, seq, d = x.shape
    mem_len = mem.shape[1]
    depth = mix_pre_g.shape[0]
    n_mixers = 2
    x2 = x.reshape(batch * seq, d)
    mem2 = mem.reshape(batch * mem_len, d)
    pos_col = positions.astype(F32).reshape(batch * seq, 1)

    for i in range(depth):
        j = i // n_mixers
        kv = _mem_kv(mem2, _row(mem_g[i]), xa_wkv[i].astype(BF16))
        xa = (_row(xa_pre_g[i]), xa_wq[i].astype(BF16), kv, xa_wo[i].astype(BF16),
              _row(xa_post_g[i]))
        ff = (_row(ffn_pre_g[i]), w_up[i].astype(BF16), w_down[i].astype(BF16),
              _row(ffn_post_g[i]))
        if i % n_mixers == 0:
            lambda_init = 0.8 - 0.6 * math.exp(-0.3 * i)
            cos, sin = _rope_tables(pos_col, DA_QK_DIM)
            qt, k2, vt = _qkv_diff(x2, _row(mix_pre_g[i]), da_w_qkv[j].astype(BF16), cos, sin,
                                   batch, seq)
            o = _diff_attention(qt, k2.reshape(batch, seq, d), vt, da_lambda[j],
                                _row(da_subln_g[j]), lambda_init)
            x2 = _layer_tail(o.reshape(batch * seq, d), da_w_o[j].astype(BF16),
                             _row(mix_post_g[i]), x2, xa, ff, seq, mem_len)
        else:
            cos, sin = _rope_tables(pos_col, HEAD_W)
            w_qkv = db_w_qkv[j].astype(BF16)
            os, lses = [], []
            for gi, (_, dil) in enumerate(DB_GROUPS):
                qkv4 = _qkv_dilated(x2, _row(mix_pre_g[i]), w_qkv, cos, sin, batch, seq, gi, dil)
                o, lse = _band_attention(qkv4)
                os.append(o)
                lses.append(lse)
            x2 = _combine_layer_tail(os, lses, db_w_o[j].astype(BF16), _row(mix_post_g[i]), x2,
                                     xa, ff, seq, mem_len)
    return x2.reshape(batch, seq, d)
```

```python
import functools
import math

import jax
import jax.numpy as jnp
from jax import lax
from jax.experimental import pallas as pl
from jax.experimental.pallas import tpu as pltpu

F32 = jnp.float32
BF16 = jnp.bfloat16

NORM_EPS = 1e-6
ROPE_THETA = 10000.0
LANES = 128
HEAD_W = 128
DA_HEADS = 8
DA_QK_DIM = 64
DA_VT_ROWS = HEAD_W + 16
DB_GROUPS = ((128, 1), (512, 4), (2048, 16))
DB_HEADS = 8
DB_RADIUS = 64
XA_HEADS = 4
XA_HEAD_DIM = 256
NEG_BIG = -1e30
VMEM_LIMIT = 48 * 1024 * 1024
TAIL_VMEM_LIMIT = 56 * 1024 * 1024
MLP_CHUNK = 1024
TAIL_SPLIT = 2

ROW_TILE = 512
ATT_TQ = 2048
ATT_TK = 512
BAND_LT = 512
BAND_QB = 128


def _params(*sem):
    return pltpu.CompilerParams(dimension_semantics=sem, vmem_limit_bytes=VMEM_LIMIT)


def _dot(a, b):
    return jnp.dot(a, b, preferred_element_type=F32)


def _dot_nt(a, b):
    return lax.dot_general(a, b, (((1,), (1,)), ((), ())), preferred_element_type=F32)


def _rms(x, g):
    ms = jnp.mean(x * x, axis=-1, keepdims=True)
    return x * lax.rsqrt(ms + NORM_EPS) * g


def _rope_slab(t, cos, sin, half):
    if half == 64:
        rot = pltpu.roll(t, 64, 1)
    else:
        lane = lax.broadcasted_iota(jnp.int32, t.shape, 1)
        rot = jnp.where((lane & 32) == 0, pltpu.roll(t, 96, 1), pltpu.roll(t, 32, 1))
    return t * cos + rot * sin


def _rope_table_kernel(pos_ref, inv_ref, sign_ref, cos_ref, sin_ref):
    ang = pos_ref[...] * inv_ref[...]
    cos_ref[...] = jnp.cos(ang)
    sin_ref[...] = jnp.sin(ang) * sign_ref[...]


def _rope_tables(pos_col, dh):
    t = pos_col.shape[0]
    inv = ROPE_THETA ** (-jnp.arange(0, dh, 2, dtype=F32) / dh)
    reps = LANES // dh
    inv_full = jnp.tile(jnp.concatenate([inv, inv]), reps).reshape(1, LANES)
    sign = jnp.tile(jnp.concatenate([-jnp.ones(dh // 2, F32), jnp.ones(dh // 2, F32)]),
                    reps).reshape(1, LANES)
    tm = min(t, 2048)
    return pl.pallas_call(
        _rope_table_kernel,
        grid=(t // tm,),
        in_specs=[pl.BlockSpec((tm, 1), lambda i: (i, 0)),
                  pl.BlockSpec((1, LANES), lambda i: (0, 0)),
                  pl.BlockSpec((1, LANES), lambda i: (0, 0))],
        out_specs=[pl.BlockSpec((tm, LANES), lambda i: (i, 0)),
                   pl.BlockSpec((tm, LANES), lambda i: (i, 0))],
        out_shape=[jax.ShapeDtypeStruct((t, LANES), F32)] * 2,
        compiler_params=_params("parallel"),
        name="rope_tables",
    )(pos_col, inv_full, sign)


def _qkv_diff_kernel(x_ref, g_ref, w_ref, cos_ref, sin_ref, qt_ref, k_ref, vt_ref, *, scale):
    d = x_ref.shape[1]
    h = _rms(x_ref[...], g_ref[...]).astype(BF16)
    cos = cos_ref[...]
    sin = sin_ref[...]
    q = _dot(h, w_ref[:, 0:d])
    for hh in range(DA_HEADS):
        sl = slice(hh * HEAD_W, (hh + 1) * HEAD_W)
        t = _rope_slab(q[:, sl], cos, sin, DA_QK_DIM // 2) * scale
        qt_ref[0, 0, sl, :] = t.T.astype(BF16)
    k = _dot(h, w_ref[:, d:2 * d])
    for hh in range(DA_HEADS):
        sl = slice(hh * HEAD_W, (hh + 1) * HEAD_W)
        k_ref[:, sl] = _rope_slab(k[:, sl], cos, sin, DA_QK_DIM // 2).astype(BF16)
    v = _dot(h, w_ref[:, 2 * d:3 * d])
    ones = jnp.ones((DA_VT_ROWS - HEAD_W, x_ref.shape[0]), BF16)
    for hh in range(DA_HEADS):
        sl = slice(hh * HEAD_W, (hh + 1) * HEAD_W)
        r0 = hh * DA_VT_ROWS
        vt_ref[0, 0, r0:r0 + HEAD_W, :] = v[:, sl].T.astype(BF16)
        vt_ref[0, 0, r0 + HEAD_W:r0 + DA_VT_ROWS, :] = ones


def _qkv_diff(x2, g, w_bf, cos, sin, batch, seq):
    t, d = x2.shape
    tm = min(ATT_TK, seq)
    nst = seq // tm
    scale = DA_QK_DIM ** -0.5 * math.log2(math.e)
    vt_rows = DA_HEADS * DA_VT_ROWS
    return pl.pallas_call(
        functools.partial(_qkv_diff_kernel, scale=scale),
        grid=(t // tm,),
        in_specs=[pl.BlockSpec((tm, d), lambda i: (i, 0)),
                  pl.BlockSpec((1, d), lambda i: (0, 0)),
                  pl.BlockSpec((d, 3 * d), lambda i: (0, 0)),
                  pl.BlockSpec((tm, LANES), lambda i: (i, 0)),
                  pl.BlockSpec((tm, LANES), lambda i: (i, 0))],
        out_specs=[pl.BlockSpec((1, 1, d, tm), lambda i: (i // nst, i % nst, 0, 0)),
                   pl.BlockSpec((tm, d), lambda i: (i, 0)),
                   pl.BlockSpec((1, 1, vt_rows, tm), lambda i: (i // nst, i % nst, 0, 0))],
        out_shape=[jax.ShapeDtypeStruct((batch, nst, d, tm), BF16),
                   jax.ShapeDtypeStruct((t, d), BF16),
                   jax.ShapeDtypeStruct((batch, nst, vt_rows, tm), BF16)],
        compiler_params=_params("parallel"),
        name="qkv_diff",
    )(x2, g, w_bf, cos, sin)


def _diff_attn_kernel(qt_ref, k_ref, vt_ref, lam_ref, g_ref, o_ref,
                      qp_ref, s0_ref, s1_ref, e0_ref, e1_ref, t0_ref, t1_ref, a0_ref, a1_ref,
                      m_ref, acc_ref, *, nk, tk, lambda_init):
    qt = jnp.concatenate([qt_ref[0, j] for j in range(qt_ref.shape[1])], axis=1)
    row = lax.broadcasted_iota(jnp.int32, qt.shape, 0)
    zero = jnp.zeros_like(qt)
    qp_ref[0] = jnp.where(row < DA_QK_DIM, qt, zero)
    qp_ref[1] = jnp.where(row >= DA_QK_DIM, qt, zero)
    m_ref[...] = jnp.full(m_ref.shape, NEG_BIG, F32)
    acc_ref[...] = jnp.zeros(acc_ref.shape, F32)
    e1_ref[...] = jnp.zeros(e1_ref.shape, BF16)
    a1_ref[...] = jnp.zeros(a1_ref.shape, F32)

    def scores(t, s_ref, tmax_ref):
        kb = k_ref[0, pl.ds(pl.multiple_of(t * tk, tk), tk), :]
        for c in range(2):
            s = _dot(kb, qp_ref[c])
            s_ref[c] = s
            tmax_ref[c] = jnp.max(s, axis=0, keepdims=True)

    def exponentiate(s_ref, tmax_ref, e_ref, alpha_ref):
        for c in range(2):
            m_prev = m_ref[c]
            m_new = jnp.maximum(m_prev, tmax_ref[c])
            alpha_ref[c] = jnp.exp2(m_prev - m_new)
            m_ref[c] = m_new
            e_ref[c] = jnp.exp2(s_ref[c] - m_new).astype(BF16)

    def accumulate(t, e_ref, alpha_ref):
        vt = vt_ref[0, t]
        for c in range(2):
            acc_ref[c] = alpha_ref[c] * acc_ref[c] + _dot(vt, e_ref[c])

    scores(0, s0_ref, t0_ref)

    def kv_pair(u, carry):
        t = 2 * u
        scores(t + 1, s1_ref, t1_ref)
        exponentiate(s0_ref, t0_ref, e0_ref, a0_ref)
        accumulate(jnp.maximum(t - 1, 0), e1_ref, a1_ref)
        scores(jnp.minimum(t + 2, nk - 1), s0_ref, t0_ref)
        exponentiate(s1_ref, t1_ref, e1_ref, a1_ref)
        accumulate(t, e0_ref, a0_ref)
        return carry

    lax.fori_loop(0, nk // 2, kv_pair, 0)
    accumulate(nk - 1, e1_ref, a1_ref)

    lam = lam_ref[...]
    lam_full = (jnp.exp(jnp.sum(lam[0:1] * lam[1:2], axis=-1, keepdims=True))
                - jnp.exp(jnp.sum(lam[2:3] * lam[3:4], axis=-1, keepdims=True))
                + lambda_init)
    inv1 = 1.0 / acc_ref[0, HEAD_W:HEAD_W + 1, :]
    inv2 = 1.0 / acc_ref[1, HEAD_W:HEAD_W + 1, :]
    ot = acc_ref[0, 0:HEAD_W, :] * inv1 - lam_full * (acc_ref[1, 0:HEAD_W, :] * inv2)
    o = _rms(ot.T, g_ref[...]) * (1.0 - lambda_init)
    o_ref[0] = o.astype(BF16)


def _diff_attention(qt, k3, vt, lam, subln_g, lambda_init):
    batch, nk, d, tk = qt.shape
    seq = k3.shape[1]
    tq = min(ATT_TQ, seq)
    nqt = tq // tk
    nq = seq // tq
    assert nk % 2 == 0 and tq % tk == 0
    return pl.pallas_call(
        functools.partial(_diff_attn_kernel, nk=nk, tk=tk, lambda_init=lambda_init),
        grid=(batch, DA_HEADS, nq),
        in_specs=[pl.BlockSpec((1, nqt, HEAD_W, tk), lambda b, h, i: (b, i, h, 0)),
                  pl.BlockSpec((1, seq, HEAD_W), lambda b, h, i: (b, 0, h)),
                  pl.BlockSpec((1, nk, DA_VT_ROWS, tk), lambda b, h, i: (b, 0, h, 0)),
                  pl.BlockSpec(lam.shape, lambda b, h, i: (0, 0)),
                  pl.BlockSpec((1, HEAD_W), lambda b, h, i: (0, 0))],
        out_specs=pl.BlockSpec((1, tq, HEAD_W), lambda b, h, i: (b, i, h)),
        out_shape=jax.ShapeDtypeStruct((batch, seq, d), BF16),
        scratch_shapes=[pltpu.VMEM((2, HEAD_W, tq), BF16),
                        pltpu.VMEM((2, tk, tq), F32),
                        pltpu.VMEM((2, tk, tq), F32),
                        pltpu.VMEM((2, tk, tq), BF16),
                        pltpu.VMEM((2, tk, tq), BF16),
                        pltpu.VMEM((2, 1, tq), F32),
                        pltpu.VMEM((2, 1, tq), F32),
                        pltpu.VMEM((2, 1, tq), F32),
                        pltpu.VMEM((2, 1, tq), F32),
                        pltpu.VMEM((2, 1, tq), F32),
                        pltpu.VMEM((2, DA_VT_ROWS, tq), F32)],
        compiler_params=_params("parallel", "parallel", "parallel"),
        name="diff_attention",
    )(qt, k3, vt, lam, subln_g)


def _mem_kv_kernel(mem_ref, g_ref, w_ref, o_ref):
    mn = _rms(mem_ref[...], g_ref[...]).astype(BF16)
    o_ref[...] = _dot(mn, w_ref[...]).astype(BF16)


def _mem_kv(mem2, g, w_bf):
    rows, d = mem2.shape
    n = w_bf.shape[1]
    tn = n // 2
    return pl.pallas_call(
        _mem_kv_kernel,
        grid=(n // tn,),
        in_specs=[pl.BlockSpec((rows, d), lambda j: (0, 0)),
                  pl.BlockSpec((1, d), lambda j: (0, 0)),
                  pl.BlockSpec((d, tn), lambda j: (0, j))],
        out_specs=pl.BlockSpec((rows, tn), lambda j: (0, j)),
        out_shape=jax.ShapeDtypeStruct((rows, n), BF16),
        compiler_params=_params("parallel"),
        name="mem_kv",
    )(mem2, g, w_bf)


def _cross_attn_heads(q, k_ref, v_ref):
    outs = []
    for hh in range(XA_HEADS):
        sl = slice(hh * XA_HEAD_DIM, (hh + 1) * XA_HEAD_DIM)
        s = _dot_nt(q[:, sl], k_ref[:, sl])
        e = jnp.exp(s - jnp.max(s, axis=-1, keepdims=True))
        p = e * (1.0 / jnp.sum(e, axis=-1, keepdims=True))
        outs.append(_dot(p.astype(BF16), v_ref[:, sl]).astype(BF16))
    return jnp.concatenate(outs, axis=1)


N_XA_REFS = 6
N_FF_REFS = 4


def _finish_layer(a, w_ref, g_ref, x_ref, rest):
    pre_g_ref, wq_ref, k_ref, v_ref, wo_ref, post_g_ref = rest[:N_XA_REFS]
    fpre_ref, wu_ref, wd_ref, fpost_ref = rest[N_XA_REFS:N_XA_REFS + N_FF_REFS]
    out_ref = rest[N_XA_REFS + N_FF_REFS]
    hs = x_ref.shape[0] // TAIL_SPLIT
    rows = [slice(p * hs, (p + 1) * hs) for p in range(TAIL_SPLIT)]
    parts = range(TAIL_SPLIT)
    y = [_dot(a[rows[p]], w_ref[...]) for p in parts]
    x = [x_ref[rows[p], :] + _rms(y[p], g_ref[...]) for p in parts]
    h = [_rms(x[p], pre_g_ref[...]).astype(BF16) for p in parts]
    q = [(_dot(h[p], wq_ref[...]) * (XA_HEAD_DIM ** -0.5)).astype(BF16) for p in parts]
    o = [_cross_attn_heads(q[p], k_ref, v_ref) for p in parts]
    y = [_dot(o[p], wo_ref[...]) for p in parts]
    x = [x[p] + _rms(y[p], post_g_ref[...]) for p in parts]
    h = [_rms(x[p], fpre_ref[...]).astype(BF16) for p in parts]
    d_ff = wu_ref.shape[1]
    chunk = min(MLP_CHUNK, d_ff)
    acc = [jnp.zeros(x[p].shape, F32) for p in parts]
    for c in range(d_ff // chunk):
        sl = slice(c * chunk, (c + 1) * chunk)
        for p in parts:
            u = jnp.maximum(_dot(h[p], wu_ref[:, sl]), 0.0)
            acc[p] = acc[p] + _dot((u * u).astype(BF16), wd_ref[sl, :])
    for p in parts:
        out_ref[rows[p], :] = x[p] + _rms(acc[p], fpost_ref[...])


def _tail_kernel(a_ref, w_ref, g_ref, x_ref, *rest):
    _finish_layer(a_ref[...], w_ref, g_ref, x_ref, rest)


def _tail_specs(tm, d, d_ff, mem_len, nst):
    def const(*shape):
        return pl.BlockSpec(shape, lambda i: (0,) * len(shape), pipeline_mode=pl.Buffered(1))

    row = pl.BlockSpec((tm, d), lambda i: (i, 0))
    in_specs = [const(d, d), const(1, d), row,
                const(1, d), const(d, d),
                pl.BlockSpec((mem_len, d), lambda i: (i // nst, 0)),
                pl.BlockSpec((mem_len, d), lambda i: (i // nst, 1)),
                const(d, d), const(1, d),
                const(1, d), const(d, d_ff), const(d_ff, d), const(1, d)]
    return in_specs, row


def _layer_tail(a2, w_o, mix_g, x2, xa, ff, seq, mem_len):
    t, d = x2.shape
    tm = min(ROW_TILE, seq)
    in_specs, row = _tail_specs(tm, d, ff[1].shape[1], mem_len, seq // tm)
    pre_g, wq, kv, wo, post_g = xa
    return pl.pallas_call(
        _tail_kernel,
        grid=(t // tm,),
        in_specs=[row] + in_specs,
        out_specs=row,
        out_shape=jax.ShapeDtypeStruct((t, d), F32),
        compiler_params=pltpu.CompilerParams(dimension_semantics=("parallel",),
                                             vmem_limit_bytes=TAIL_VMEM_LIMIT),
        name="layer_tail",
    )(a2, w_o, mix_g, x2, pre_g, wq, kv, kv, wo, post_g, *ff)


def _qkv_dil_kernel(x_ref, g_ref, wq_ref, wk_ref, wv_ref, cos_ref, sin_ref, o_ref, *stage,
                    scale, dil):
    tm, d = x_ref.shape
    lt = tm // dil
    n_slab = d // HEAD_W
    hn = _rms(x_ref[...], g_ref[...])
    if dil == 1:
        h = hn.astype(BF16)
        cos = cos_ref[...]
        sin = sin_ref[...]
    else:
        f_ref, hp_ref, tp_ref = stage
        for hh in range(n_slab):
            f_ref[hh] = hn[:, hh * HEAD_W:(hh + 1) * HEAD_W]
        f_ref[n_slab] = cos_ref[...]
        f_ref[n_slab + 1] = sin_ref[...]
        for r in range(dil):
            src = pl.ds(r, lt, stride=dil)
            dst = slice(r * lt, (r + 1) * lt)
            for hh in range(n_slab):
                hp_ref[dst, hh * HEAD_W:(hh + 1) * HEAD_W] = f_ref[hh, src, :].astype(BF16)
            tp_ref[0, dst, :] = f_ref[n_slab, src, :]
            tp_ref[1, dst, :] = f_ref[n_slab + 1, src, :]
        h = hp_ref[...]
        cos = tp_ref[0]
        sin = tp_ref[1]
    for which in range(3):
        y = _dot(h, (wq_ref, wk_ref, wv_ref)[which][...])
        for hh in range(n_slab):
            col = slice(which * d + hh * HEAD_W, which * d + (hh + 1) * HEAD_W)
            t = y[:, hh * HEAD_W:(hh + 1) * HEAD_W]
            if which < 2:
                t = _rope_slab(t, cos, sin, HEAD_W // 2)
            if which == 0:
                t = t * scale
            t = t.astype(BF16)
            for r in range(dil):
                o_ref[0, r, :, col] = t[r * lt:(r + 1) * lt]


def _qkv_dilated(x2, g, w_bf, cos, sin, batch, seq, group, dil):
    groups = len(DB_GROUPS)
    t, d = x2.shape
    tm = min(ROW_TILE, seq)
    nst = seq // tm
    lt = tm // dil
    stage = []
    if dil > 1:
        stage = [pltpu.VMEM((d // HEAD_W + 2, tm, HEAD_W), F32),
                 pltpu.VMEM((tm, d), BF16),
                 pltpu.VMEM((2, tm, HEAD_W), F32)]
    return pl.pallas_call(
        functools.partial(_qkv_dil_kernel, scale=HEAD_W ** -0.5, dil=dil),
        grid=(t // tm,),
        in_specs=[pl.BlockSpec((tm, d), lambda i: (i, 0)),
                  pl.BlockSpec((1, d), lambda i: (0, 0)),
                  pl.BlockSpec((d, d), lambda i: (0, group)),
                  pl.BlockSpec((d, d), lambda i: (0, groups + group)),
                  pl.BlockSpec((d, d), lambda i: (0, 2 * groups + group)),
                  pl.BlockSpec((tm, LANES), lambda i: (i, 0)),
                  pl.BlockSpec((tm, LANES), lambda i: (i, 0))],
        out_specs=pl.BlockSpec((1, dil, lt, 3 * d), lambda i: (i // nst, 0, i % nst, 0)),
        out_shape=jax.ShapeDtypeStruct((batch, dil, seq // dil, 3 * d), BF16),
        scratch_shapes=stage,
        compiler_params=_params("parallel"),
        name=f"qkv_dilated_d{dil}",
    )(x2, g, w_bf, w_bf, w_bf, cos, sin)


def _band_attn_kernel(q_ref, km_ref, kl_ref, kr_ref, vm_ref, vl_ref, vr_ref, o_ref, lse_ref,
                      kc_ref, vc_ref, *, lt, sub_len):
    i = pl.program_id(2)
    r = DB_RADIUS

    @pl.when((pl.program_id(0) == 0) & (pl.program_id(1) == 0) & (i == 0))
    def _():
        vc_ref[...] = jnp.ones(vc_ref.shape, BF16)

    kc_ref[0:r] = kl_ref[0, 0]
    kc_ref[r:r + lt] = km_ref[0, 0]
    kc_ref[r + lt:r + lt + r] = kr_ref[0, 0]
    for hh in range(DB_HEADS):
        sl = slice(hh * HEAD_W, (hh + 1) * HEAD_W)
        dst = slice(2 * hh * HEAD_W, (2 * hh + 1) * HEAD_W)
        vc_ref[0:r, dst] = vl_ref[0, 0, :, sl]
        vc_ref[r:r + lt, dst] = vm_ref[0, 0, :, sl]
        vc_ref[r + lt:r + lt + r, dst] = vr_ref[0, 0, :, sl]

    qb = BAND_QB
    kb = qb + 2 * r
    qi = lax.broadcasted_iota(jnp.int32, (qb, kb), 0)
    ki = lax.broadcasted_iota(jnp.int32, (qb, kb), 1)
    band = jnp.abs(ki - r - qi) <= r
    lane = lax.broadcasted_iota(jnp.int32, (qb, LANES), 1)

    for j in range(lt // qb):
        q0 = j * qb
        key_pos = i * lt + (q0 - r) + ki
        mask = band & (key_pos >= 0) & (key_pos < sub_len)
        lse_all = jnp.zeros((qb, LANES), F32)
        for hh in range(DB_HEADS):
            sl = slice(hh * HEAD_W, (hh + 1) * HEAD_W)
            s = _dot_nt(q_ref[0, 0, q0:q0 + qb, sl], kc_ref[q0:q0 + kb, sl])
            s = jnp.where(mask, s, NEG_BIG)
            m = jnp.max(s, axis=-1, keepdims=True)
            p = jnp.exp(s - m).astype(BF16)
            oa = _dot(p, vc_ref[q0:q0 + kb, 2 * hh * HEAD_W:(2 * hh + 2) * HEAD_W])
            den = oa[:, HEAD_W:]
            o_ref[0, 0, q0:q0 + qb, sl] = (oa[:, :HEAD_W] * (1.0 / den)).astype(BF16)
            lse_all = jnp.where(lane == hh, m + jnp.log(den), lse_all)
        lse_ref[0, 0, q0:q0 + qb, :] = lse_all


def _band_attention(qkv4):
    batch, dil, sub_len, n = qkv4.shape
    d = n // 3
    lt = min(BAND_LT, sub_len)
    r = DB_RADIUS
    hpt = lt // r
    n_halo = sub_len // r

    def main_spec(which):
        return pl.BlockSpec((1, 1, lt, d), lambda b, rr, i: (b, rr, i, which))

    def left_spec(which):
        return pl.BlockSpec((1, 1, r, d), lambda b, rr, i: (
            b, rr, jnp.maximum(i * hpt - 1, 0), which))

    def right_spec(which):
        return pl.BlockSpec((1, 1, r, d), lambda b, rr, i: (
            b, rr, jnp.minimum((i + 1) * hpt, n_halo - 1), which))

    return pl.pallas_call(
        functools.partial(_band_attn_kernel, lt=lt, sub_len=sub_len),
        grid=(batch, dil, sub_len // lt),
        in_specs=[main_spec(0), main_spec(1), left_spec(1), right_spec(1),
                  main_spec(2), left_spec(2), right_spec(2)],
        out_specs=[pl.BlockSpec((1, 1, lt, d), lambda b, rr, i: (b, rr, i, 0)),
                   pl.BlockSpec((1, 1, lt, LANES), lambda b, rr, i: (b, rr, i, 0))],
        out_shape=[jax.ShapeDtypeStruct((batch, dil, sub_len, d), BF16),
                   jax.ShapeDtypeStruct((batch, dil, sub_len, LANES), F32)],
        scratch_shapes=[pltpu.VMEM((lt + 2 * r, d), BF16),
                        pltpu.VMEM((lt + 2 * r, 2 * d), BF16)],
        compiler_params=_params("arbitrary", "arbitrary", "arbitrary"),
        name=f"band_attention_d{dil}",
    )(qkv4, qkv4, qkv4, qkv4, qkv4, qkv4, qkv4)


def _combine_tail_kernel(o0_ref, o1_ref, o2_ref, l0_ref, l1_ref, l2_ref, w_ref, g_ref, x_ref,
                         *rest, dils):
    tm = x_ref.shape[0]
    n_slab = x_ref.shape[1] // HEAD_W
    o_refs = (o0_ref, o1_ref, o2_ref)
    l_refs = (l0_ref, l1_ref, l2_ref)
    stage = list(rest[N_XA_REFS + N_FF_REFS + 1:])
    o_tok, l_tok = [], []
    for gi, dil in enumerate(dils):
        if dil == 1:
            o_tok.append(None)
            l_tok.append(l_refs[gi][0, 0])
            continue
        so_ref = stage.pop(0)
        sl_ref = stage.pop(0)
        for r in range(dil):
            rows = pl.ds(r, tm // dil, stride=dil)
            sl_ref[rows, :] = l_refs[gi][0, r]
            for hh in range(n_slab):
                so_ref[hh, rows, :] = o_refs[gi][0, r, :, hh * HEAD_W:(hh + 1) * HEAD_W].astype(F32)
        o_tok.append(so_ref)
        l_tok.append(sl_ref[...])
    mx = jnp.maximum(jnp.maximum(l_tok[0], l_tok[1]), l_tok[2])
    es = [jnp.exp(l - mx) for l in l_tok]
    inv = 1.0 / (es[0] + es[1] + es[2])
    ws = [e * inv for e in es]
    outs = []
    for hh in range(n_slab):
        sl = slice(hh * HEAD_W, (hh + 1) * HEAD_W)
        c = slice(hh, hh + 1)
        oh = None
        for gi, dil in enumerate(dils):
            og = o_refs[gi][0, 0, :, sl].astype(F32) if dil == 1 else o_tok[gi][hh]
            term = ws[gi][:, c] * og
            oh = term if oh is None else oh + term
        outs.append(oh.astype(BF16))
    _finish_layer(jnp.concatenate(outs, axis=1), w_ref, g_ref, x_ref, rest)


def _combine_layer_tail(os, lses, w_o, mix_g, x2, xa, ff, seq, mem_len):
    t, d = x2.shape
    tm = min(ROW_TILE, seq)
    nst = seq // tm
    dils = tuple(o.shape[1] for o in os)

    def group_spec(dil, width):
        return pl.BlockSpec((1, dil, tm // dil, width), lambda i: (i // nst, 0, i % nst, 0))

    stage = []
    for dil in dils:
        if dil > 1:
            stage += [pltpu.VMEM((d // HEAD_W, tm, HEAD_W), F32), pltpu.VMEM((tm, LANES), F32)]
    in_specs, row = _tail_specs(tm, d, ff[1].shape[1], mem_len, nst)
    pre_g, wq, kv, wo, post_g = xa
    return pl.pallas_call(
        functools.partial(_combine_tail_kernel, dils=dils),
        grid=(t // tm,),
        in_specs=[group_spec(dil, d) for dil in dils] + [group_spec(dil, LANES) for dil in dils]
                 + in_specs,
        out_specs=row,
        out_shape=jax.ShapeDtypeStruct((t, d), F32),
        scratch_shapes=stage,
        compiler_params=pltpu.CompilerParams(dimension_semantics=("parallel",),
                                             vmem_limit_bytes=TAIL_VMEM_LIMIT),
        name="combine_layer_tail",
    )(*os, *lses, w_o, mix_g, x2, pre_g, wq, kv, kv, wo, post_g, *ff)


def _row(v):
    return v.reshape(1, -1)


def kernel(x, mem, positions, mix_pre_g, mix_post_g, da_w_qkv, da_lambda, da_subln_g, da_w_o,
           db_w_qkv, db_w_o, xa_pre_g, mem_g, xa_wq, xa_wkv, xa_wo, xa_post_g,
           ffn_pre_g, w_up, w_down, ffn_post_g):
    batch, seq, d = x.shape
    mem_len = mem.shape[1]
    depth = mix_pre_g.shape[0]
    n_mixers = 2
    x2 = x.reshape(batch * seq, d)
    mem2 = mem.reshape(batch * mem_len, d)
    pos_col = positions.astype(F32).reshape(batch * seq, 1)

    for i in range(depth):
        j = i // n_mixers
        kv = _mem_kv(mem2, _row(mem_g[i]), xa_wkv[i].astype(BF16))
        xa = (_row(xa_pre_g[i]), xa_wq[i].astype(BF16), kv, xa_wo[i].astype(BF16),
              _row(xa_post_g[i]))
        ff = (_row(ffn_pre_g[i]), w_up[i].astype(BF16), w_down[i].astype(BF16),
              _row(ffn_post_g[i]))
        if i % n_mixers == 0:
            lambda_init = 0.8 - 0.6 * math.exp(-0.3 * i)
            cos, sin = _rope_tables(pos_col, DA_QK_DIM)
            qt, k2, vt = _qkv_diff(x2, _row(mix_pre_g[i]), da_w_qkv[j].astype(BF16), cos, sin,
                                   batch, seq)
            o = _diff_attention(qt, k2.reshape(batch, seq, d), vt, da_lambda[j],
                                _row(da_subln_g[j]), lambda_init)
            x2 = _layer_tail(o.reshape(batch * seq, d), da_w_o[j].astype(BF16),
                             _row(mix_post_g[i]), x2, xa, ff, seq, mem_len)
        else:
            cos, sin = _rope_tables(pos_col, HEAD_W)
            w_qkv = db_w_qkv[j].astype(BF16)
            os, lses = [], []
            for gi, (_, dil) in enumerate(DB_GROUPS):
                qkv4 = _qkv_dilated(x2, _row(mix_pre_g[i]), w_qkv, cos, sin, batch, seq, gi, dil)
                o, lse = _band_attention(qkv4)
                os.append(o)
                lses.append(lse)
            x2 = _combine_layer_tail(os, lses, db_w_o[j].astype(BF16), _row(mix_post_g[i]), x2,
                                     xa, ff, seq, mem_len)
    return x2.reshape(batch, seq, d)
```

```python
import functools
import math

import jax
import jax.numpy as jnp
from jax import lax
from jax.experimental import pallas as pl
from jax.experimental.pallas import tpu as pltpu

F32 = jnp.float32
BF16 = jnp.bfloat16

NORM_EPS = 1e-6
ROPE_THETA = 10000.0
LANES = 128
HEAD_W = 128
DA_HEADS = 8
DA_QK_DIM = 64
DA_VT_ROWS = HEAD_W + 16
DB_GROUPS = ((128, 1), (512, 4), (2048, 16))
DB_HEADS = 8
DB_RADIUS = 64
XA_HEADS = 4
XA_HEAD_DIM = 256
NEG_BIG = -1e30
VMEM_LIMIT = 48 * 1024 * 1024
TAIL_VMEM_LIMIT = 56 * 1024 * 1024
MLP_CHUNK = 1024
TAIL_SPLIT = 2

ROW_TILE = 512
ATT_TQ = 2048
ATT_TK = 512
BAND_LT = 512
BAND_QB = 128


def _params(*sem):
    return pltpu.CompilerParams(dimension_semantics=sem, vmem_limit_bytes=VMEM_LIMIT)


def _dot(a, b):
    return jnp.dot(a, b, preferred_element_type=F32)


def _dot_nt(a, b):
    return lax.dot_general(a, b, (((1,), (1,)), ((), ())), preferred_element_type=F32)


def _rms(x, g):
    ms = jnp.mean(x * x, axis=-1, keepdims=True)
    return x * lax.rsqrt(ms + NORM_EPS) * g


def _rope_slab(t, cos, sin, half):
    if half == 64:
        rot = pltpu.roll(t, 64, 1)
    else:
        lane = lax.broadcasted_iota(jnp.int32, t.shape, 1)
        rot = jnp.where((lane & 32) == 0, pltpu.roll(t, 96, 1), pltpu.roll(t, 32, 1))
    return t * cos + rot * sin


def _rope_table_kernel(pos_ref, inv_ref, sign_ref, cos_ref, sin_ref):
    ang = pos_ref[...] * inv_ref[...]
    cos_ref[...] = jnp.cos(ang)
    sin_ref[...] = jnp.sin(ang) * sign_ref[...]


def _rope_tables(pos_col, dh):
    t = pos_col.shape[0]
    inv = ROPE_THETA ** (-jnp.arange(0, dh, 2, dtype=F32) / dh)
    reps = LANES // dh
    inv_full = jnp.tile(jnp.concatenate([inv, inv]), reps).reshape(1, LANES)
    sign = jnp.tile(jnp.concatenate([-jnp.ones(dh // 2, F32), jnp.ones(dh // 2, F32)]),
                    reps).reshape(1, LANES)
    tm = min(t, 2048)
    return pl.pallas_call(
        _rope_table_kernel,
        grid=(t // tm,),
        in_specs=[pl.BlockSpec((tm, 1), lambda i: (i, 0)),
                  pl.BlockSpec((1, LANES), lambda i: (0, 0)),
                  pl.BlockSpec((1, LANES), lambda i: (0, 0))],
        out_specs=[pl.BlockSpec((tm, LANES), lambda i: (i, 0)),
                   pl.BlockSpec((tm, LANES), lambda i: (i, 0))],
        out_shape=[jax.ShapeDtypeStruct((t, LANES), F32)] * 2,
        compiler_params=_params("parallel"),
        name="rope_tables",
    )(pos_col, inv_full, sign)


def _qkv_diff_kernel(x_ref, g_ref, w_ref, cos_ref, sin_ref, qt_ref, k_ref, vt_ref, *, scale):
    d = x_ref.shape[1]
    h = _rms(x_ref[...], g_ref[...]).astype(BF16)
    cos = cos_ref[...]
    sin = sin_ref[...]
    q = _dot(h, w_ref[:, 0:d])
    for hh in range(DA_HEADS):
        sl = slice(hh * HEAD_W, (hh + 1) * HEAD_W)
        t = _rope_slab(q[:, sl], cos, sin, DA_QK_DIM // 2) * scale
        qt_ref[0, 0, sl, :] = t.T.astype(BF16)
    k = _dot(h, w_ref[:, d:2 * d])
    for hh in range(DA_HEADS):
        sl = slice(hh * HEAD_W, (hh + 1) * HEAD_W)
        k_ref[:, sl] = _rope_slab(k[:, sl], cos, sin, DA_QK_DIM // 2).astype(BF16)
    v = _dot(h, w_ref[:, 2 * d:3 * d])
    ones = jnp.ones((DA_VT_ROWS - HEAD_W, x_ref.shape[0]), BF16)
    for hh in range(DA_HEADS):
        sl = slice(hh * HEAD_W, (hh + 1) * HEAD_W)
        r0 = hh * DA_VT_ROWS
        vt_ref[0, 0, r0:r0 + HEAD_W, :] = v[:, sl].T.astype(BF16)
        vt_ref[0, 0, r0 + HEAD_W:r0 + DA_VT_ROWS, :] = ones


def _qkv_diff(x2, g, w_bf, cos, sin, batch, seq):
    t, d = x2.shape
    tm = min(ATT_TK, seq)
    nst = seq // tm
    scale = DA_QK_DIM ** -0.5 * math.log2(math.e)
    vt_rows = DA_HEADS * DA_VT_ROWS
    return pl.pallas_call(
        functools.partial(_qkv_diff_kernel, scale=scale),
        grid=(t // tm,),
        in_specs=[pl.BlockSpec((tm, d), lambda i: (i, 0)),
                  pl.BlockSpec((1, d), lambda i: (0, 0)),
                  pl.BlockSpec((d, 3 * d), lambda i: (0, 0)),
                  pl.BlockSpec((tm, LANES), lambda i: (i, 0)),
                  pl.BlockSpec((tm, LANES), lambda i: (i, 0))],
        out_specs=[pl.BlockSpec((1, 1, d, tm), lambda i: (i // nst, i % nst, 0, 0)),
                   pl.BlockSpec((tm, d), lambda i: (i, 0)),
                   pl.BlockSpec((1, 1, vt_rows, tm), lambda i: (i // nst, i % nst, 0, 0))],
        out_shape=[jax.ShapeDtypeStruct((batch, nst, d, tm), BF16),
                   jax.ShapeDtypeStruct((t, d), BF16),
                   jax.ShapeDtypeStruct((batch, nst, vt_rows, tm), BF16)],
        compiler_params=_params("parallel"),
        name="qkv_diff",
    )(x2, g, w_bf, cos, sin)


def _diff_attn_kernel(qt_ref, k_ref, vt_ref, lam_ref, g_ref, o_ref,
                      qp_ref, s0_ref, s1_ref, e0_ref, e1_ref, t0_ref, t1_ref, a0_ref, a1_ref,
                      m_ref, acc_ref, *, nk, tk, lambda_init):
    qt = jnp.concatenate([qt_ref[0, j] for j in range(qt_ref.shape[1])], axis=1)
    row = lax.broadcasted_iota(jnp.int32, qt.shape, 0)
    zero = jnp.zeros_like(qt)
    qp_ref[0] = jnp.where(row < DA_QK_DIM, qt, zero)
    qp_ref[1] = jnp.where(row >= DA_QK_DIM, qt, zero)
    m_ref[...] = jnp.full(m_ref.shape, NEG_BIG, F32)
    acc_ref[...] = jnp.zeros(acc_ref.shape, F32)

    def scores(t, s_ref, tmax_ref):
        kb = k_ref[0, pl.ds(pl.multiple_of(t * tk, tk), tk), :]
        for c in range(2):
            s = _dot(kb, qp_ref[c])
            s_ref[c] = s
            tmax_ref[c] = jnp.max(s, axis=0, keepdims=True)

    def exponentiate(s_ref, tmax_ref, e_ref, alpha_ref):
        for c in range(2):
            m_prev = m_ref[c]
            m_new = jnp.maximum(m_prev, tmax_ref[c])
            alpha_ref[c] = jnp.exp2(m_prev - m_new)
            m_ref[c] = m_new
            e_ref[c] = jnp.exp2(s_ref[c] - m_new).astype(BF16)

    def accumulate(t, e_ref, alpha_ref):
        vt = vt_ref[0, t]
        for c in range(2):
            acc_ref[c] = alpha_ref[c] * acc_ref[c] + _dot(vt, e_ref[c])

    def half_step(t, s_next, tmax_next, s_cur, tmax_cur, e_cur, a_cur, e_prev, a_prev,
                  with_scores=True, with_acc=True):
        if with_scores:
            scores(t + 1, s_next, tmax_next)
        exponentiate(s_cur, tmax_cur, e_cur, a_cur)
        if with_acc:
            accumulate(t - 1, e_prev, a_prev)

    even = (s1_ref, t1_ref, s0_ref, t0_ref, e0_ref, a0_ref, e1_ref, a1_ref)
    odd = (s0_ref, t0_ref, s1_ref, t1_ref, e1_ref, a1_ref, e0_ref, a0_ref)

    scores(0, s0_ref, t0_ref)
    half_step(0, *even, with_acc=False)
    half_step(1, *odd)

    def kv_pair(u, carry):
        t = 2 * u
        half_step(t, *even)
        half_step(t + 1, *odd)
        return carry

    lax.fori_loop(1, nk // 2 - 1, kv_pair, 0)
    half_step(nk - 2, *even)
    half_step(nk - 1, *odd, with_scores=False)
    accumulate(nk - 1, e1_ref, a1_ref)

    lam = lam_ref[...]
    lam_full = (jnp.exp(jnp.sum(lam[0:1] * lam[1:2], axis=-1, keepdims=True))
                - jnp.exp(jnp.sum(lam[2:3] * lam[3:4], axis=-1, keepdims=True))
                + lambda_init)
    inv1 = 1.0 / acc_ref[0, HEAD_W:HEAD_W + 1, :]
    inv2 = 1.0 / acc_ref[1, HEAD_W:HEAD_W + 1, :]
    ot = acc_ref[0, 0:HEAD_W, :] * inv1 - lam_full * (acc_ref[1, 0:HEAD_W, :] * inv2)
    o = _rms(ot.T, g_ref[...]) * (1.0 - lambda_init)
    o_ref[0] = o.astype(BF16)


def _diff_attention(qt, k3, vt, lam, subln_g, lambda_init):
    batch, nk, d, tk = qt.shape
    seq = k3.shape[1]
    tq = min(ATT_TQ, seq)
    nqt = tq // tk
    nq = seq // tq
    assert nk % 2 == 0 and nk >= 4 and tq % tk == 0
    return pl.pallas_call(
        functools.partial(_diff_attn_kernel, nk=nk, tk=tk, lambda_init=lambda_init),
        grid=(batch, DA_HEADS, nq),
        in_specs=[pl.BlockSpec((1, nqt, HEAD_W, tk), lambda b, h, i: (b, i, h, 0)),
                  pl.BlockSpec((1, seq, HEAD_W), lambda b, h, i: (b, 0, h)),
                  pl.BlockSpec((1, nk, DA_VT_ROWS, tk), lambda b, h, i: (b, 0, h, 0)),
                  pl.BlockSpec(lam.shape, lambda b, h, i: (0, 0)),
                  pl.BlockSpec((1, HEAD_W), lambda b, h, i: (0, 0))],
        out_specs=pl.BlockSpec((1, tq, HEAD_W), lambda b, h, i: (b, i, h)),
        out_shape=jax.ShapeDtypeStruct((batch, seq, d), BF16),
        scratch_shapes=[pltpu.VMEM((2, HEAD_W, tq), BF16),
                        pltpu.VMEM((2, tk, tq), F32),
                        pltpu.VMEM((2, tk, tq), F32),
                        pltpu.VMEM((2, tk, tq), BF16),
                        pltpu.VMEM((2, tk, tq), BF16),
                        pltpu.VMEM((2, 1, tq), F32),
                        pltpu.VMEM((2, 1, tq), F32),
                        pltpu.VMEM((2, 1, tq), F32),
                        pltpu.VMEM((2, 1, tq), F32),
                        pltpu.VMEM((2, 1, tq), F32),
                        pltpu.VMEM((2, DA_VT_ROWS, tq), F32)],
        compiler_params=_params("parallel", "parallel", "parallel"),
        name="diff_attention",
    )(qt, k3, vt, lam, subln_g)


def _mem_kv_kernel(mem_ref, g_ref, w_ref, o_ref):
    mn = _rms(mem_ref[...], g_ref[...]).astype(BF16)
    o_ref[...] = _dot(mn, w_ref[...]).astype(BF16)


def _mem_kv(mem2, g, w_bf):
    rows, d = mem2.shape
    n = w_bf.shape[1]
    tn = n // 2
    return pl.pallas_call(
        _mem_kv_kernel,
        grid=(n // tn,),
        in_specs=[pl.BlockSpec((rows, d), lambda j: (0, 0)),
                  pl.BlockSpec((1, d), lambda j: (0, 0)),
                  pl.BlockSpec((d, tn), lambda j: (0, j))],
        out_specs=pl.BlockSpec((rows, tn), lambda j: (0, j)),
        out_shape=jax.ShapeDtypeStruct((rows, n), BF16),
        compiler_params=_params("parallel"),
        name="mem_kv",
    )(mem2, g, w_bf)


def _cross_attn_heads(q, k_ref, v_ref):
    outs = []
    for hh in range(XA_HEADS):
        sl = slice(hh * XA_HEAD_DIM, (hh + 1) * XA_HEAD_DIM)
        s = _dot_nt(q[:, sl], k_ref[:, sl])
        e = jnp.exp(s - jnp.max(s, axis=-1, keepdims=True))
        p = e * (1.0 / jnp.sum(e, axis=-1, keepdims=True))
        outs.append(_dot(p.astype(BF16), v_ref[:, sl]).astype(BF16))
    return jnp.concatenate(outs, axis=1)


N_XA_REFS = 6
N_FF_REFS = 4


def _finish_layer(a, w_ref, g_ref, x_ref, rest):
    pre_g_ref, wq_ref, k_ref, v_ref, wo_ref, post_g_ref = rest[:N_XA_REFS]
    fpre_ref, wu_ref, wd_ref, fpost_ref = rest[N_XA_REFS:N_XA_REFS + N_FF_REFS]
    out_ref = rest[N_XA_REFS + N_FF_REFS]
    hs = x_ref.shape[0] // TAIL_SPLIT
    rows = [slice(p * hs, (p + 1) * hs) for p in range(TAIL_SPLIT)]
    parts = range(TAIL_SPLIT)
    y = [_dot(a[rows[p]], w_ref[...]) for p in parts]
    x = [x_ref[rows[p], :] + _rms(y[p], g_ref[...]) for p in parts]
    h = [_rms(x[p], pre_g_ref[...]).astype(BF16) for p in parts]
    q = [(_dot(h[p], wq_ref[...]) * (XA_HEAD_DIM ** -0.5)).astype(BF16) for p in parts]
    o = [_cross_attn_heads(q[p], k_ref, v_ref) for p in parts]
    y = [_dot(o[p], wo_ref[...]) for p in parts]
    x = [x[p] + _rms(y[p], post_g_ref[...]) for p in parts]
    h = [_rms(x[p], fpre_ref[...]).astype(BF16) for p in parts]
    d_ff = wu_ref.shape[1]
    chunk = min(MLP_CHUNK, d_ff)
    acc = [jnp.zeros(x[p].shape, F32) for p in parts]
    for c in range(d_ff // chunk):
        sl = slice(c * chunk, (c + 1) * chunk)
        for p in parts:
            u = jnp.maximum(_dot(h[p], wu_ref[:, sl]), 0.0)
            acc[p] = acc[p] + _dot((u * u).astype(BF16), wd_ref[sl, :])
    for p in parts:
        out_ref[rows[p], :] = x[p] + _rms(acc[p], fpost_ref[...])


def _tail_kernel(a_ref, w_ref, g_ref, x_ref, *rest):
    _finish_layer(a_ref[...], w_ref, g_ref, x_ref, rest)


def _tail_specs(tm, d, d_ff, mem_len, nst):
    def const(*shape):
        return pl.BlockSpec(shape, lambda i: (0,) * len(shape), pipeline_mode=pl.Buffered(1))

    row = pl.BlockSpec((tm, d), lambda i: (i, 0))
    in_specs = [const(d, d), const(1, d), row,
                const(1, d), const(d, d),
                pl.BlockSpec((mem_len, d), lambda i: (i // nst, 0)),
                pl.BlockSpec((mem_len, d), lambda i: (i // nst, 1)),
                const(d, d), const(1, d),
                const(1, d), const(d, d_ff), const(d_ff, d), const(1, d)]
    return in_specs, row


def _layer_tail(a2, w_o, mix_g, x2, xa, ff, seq, mem_len):
    t, d = x2.shape
    tm = min(ROW_TILE, seq)
    in_specs, row = _tail_specs(tm, d, ff[1].shape[1], mem_len, seq // tm)
    pre_g, wq, kv, wo, post_g = xa
    return pl.pallas_call(
        _tail_kernel,
        grid=(t // tm,),
        in_specs=[row] + in_specs,
        out_specs=row,
        out_shape=jax.ShapeDtypeStruct((t, d), F32),
        compiler_params=pltpu.CompilerParams(dimension_semantics=("parallel",),
                                             vmem_limit_bytes=TAIL_VMEM_LIMIT),
        name="layer_tail",
    )(a2, w_o, mix_g, x2, pre_g, wq, kv, kv, wo, post_g, *ff)


def _qkv_dil_kernel(x_ref, g_ref, wq_ref, wk_ref, wv_ref, cos_ref, sin_ref, o_ref, *stage,
                    scale, dil):
    tm, d = x_ref.shape
    lt = tm // dil
    n_slab = d // HEAD_W
    hn = _rms(x_ref[...], g_ref[...])
    if dil == 1:
        h = hn.astype(BF16)
        cos = cos_ref[...]
        sin = sin_ref[...]
    else:
        f_ref, hp_ref, tp_ref = stage
        for hh in range(n_slab):
            f_ref[hh] = hn[:, hh * HEAD_W:(hh + 1) * HEAD_W]
        f_ref[n_slab] = cos_ref[...]
        f_ref[n_slab + 1] = sin_ref[...]
        for r in range(dil):
            src = pl.ds(r, lt, stride=dil)
            dst = slice(r * lt, (r + 1) * lt)
            for hh in range(n_slab):
                hp_ref[dst, hh * HEAD_W:(hh + 1) * HEAD_W] = f_ref[hh, src, :].astype(BF16)
            tp_ref[0, dst, :] = f_ref[n_slab, src, :]
            tp_ref[1, dst, :] = f_ref[n_slab + 1, src, :]
        h = hp_ref[...]
        cos = tp_ref[0]
        sin = tp_ref[1]
    for which in range(3):
        y = _dot(h, (wq_ref, wk_ref, wv_ref)[which][...])
        for hh in range(n_slab):
            col = slice(which * d + hh * HEAD_W, which * d + (hh + 1) * HEAD_W)
            t = y[:, hh * HEAD_W:(hh + 1) * HEAD_W]
            if which < 2:
                t = _rope_slab(t, cos, sin, HEAD_W // 2)
            if which == 0:
                t = t * scale
            t = t.astype(BF16)
            for r in range(dil):
                o_ref[0, r, :, col] = t[r * lt:(r + 1) * lt]


def _qkv_dilated(x2, g, w_bf, cos, sin, batch, seq, group, dil):
    groups = len(DB_GROUPS)
    t, d = x2.shape
    tm = min(ROW_TILE, seq)
    nst = seq // tm
    lt = tm // dil
    stage = []
    if dil > 1:
        stage = [pltpu.VMEM((d // HEAD_W + 2, tm, HEAD_W), F32),
                 pltpu.VMEM((tm, d), BF16),
                 pltpu.VMEM((2, tm, HEAD_W), F32)]
    return pl.pallas_call(
        functools.partial(_qkv_dil_kernel, scale=HEAD_W ** -0.5, dil=dil),
        grid=(t // tm,),
        in_specs=[pl.BlockSpec((tm, d), lambda i: (i, 0)),
                  pl.BlockSpec((1, d), lambda i: (0, 0)),
                  pl.BlockSpec((d, d), lambda i: (0, group)),
                  pl.BlockSpec((d, d), lambda i: (0, groups + group)),
                  pl.BlockSpec((d, d), lambda i: (0, 2 * groups + group)),
                  pl.BlockSpec((tm, LANES), lambda i: (i, 0)),
                  pl.BlockSpec((tm, LANES), lambda i: (i, 0))],
        out_specs=pl.BlockSpec((1, dil, lt, 3 * d), lambda i: (i // nst, 0, i % nst, 0)),
        out_shape=jax.ShapeDtypeStruct((batch, dil, seq // dil, 3 * d), BF16),
        scratch_shapes=stage,
        compiler_params=_params("parallel"),
        name=f"qkv_dilated_d{dil}",
    )(x2, g, w_bf, w_bf, w_bf, cos, sin)


def _band_attn_kernel(q_ref, km_ref, kl_ref, kr_ref, vm_ref, vl_ref, vr_ref, o_ref, lse_ref,
                      kc_ref, vc_ref, *, lt, sub_len):
    i = pl.program_id(2)
    r = DB_RADIUS

    @pl.when((pl.program_id(0) == 0) & (pl.program_id(1) == 0) & (i == 0))
    def _():
        vc_ref[...] = jnp.ones(vc_ref.shape, BF16)

    kc_ref[0:r] = kl_ref[0, 0]
    kc_ref[r:r + lt] = km_ref[0, 0]
    kc_ref[r + lt:r + lt + r] = kr_ref[0, 0]
    for hh in range(DB_HEADS):
        sl = slice(hh * HEAD_W, (hh + 1) * HEAD_W)
        dst = slice(2 * hh * HEAD_W, (2 * hh + 1) * HEAD_W)
        vc_ref[0:r, dst] = vl_ref[0, 0, :, sl]
        vc_ref[r:r + lt, dst] = vm_ref[0, 0, :, sl]
        vc_ref[r + lt:r + lt + r, dst] = vr_ref[0, 0, :, sl]

    qb = BAND_QB
    kb = qb + 2 * r
    qi = lax.broadcasted_iota(jnp.int32, (qb, kb), 0)
    ki = lax.broadcasted_iota(jnp.int32, (qb, kb), 1)
    band = jnp.abs(ki - r - qi) <= r
    lane = lax.broadcasted_iota(jnp.int32, (qb, LANES), 1)

    for j in range(lt // qb):
        q0 = j * qb
        key_pos = i * lt + (q0 - r) + ki
        mask = band & (key_pos >= 0) & (key_pos < sub_len)
        lse_all = jnp.zeros((qb, LANES), F32)
        for hh in range(DB_HEADS):
            sl = slice(hh * HEAD_W, (hh + 1) * HEAD_W)
            s = _dot_nt(q_ref[0, 0, q0:q0 + qb, sl], kc_ref[q0:q0 + kb, sl])
            s = jnp.where(mask, s, NEG_BIG)
            m = jnp.max(s, axis=-1, keepdims=True)
            p = jnp.exp(s - m).astype(BF16)
            oa = _dot(p, vc_ref[q0:q0 + kb, 2 * hh * HEAD_W:(2 * hh + 2) * HEAD_W])
            den = oa[:, HEAD_W:]
            o_ref[0, 0, q0:q0 + qb, sl] = (oa[:, :HEAD_W] * (1.0 / den)).astype(BF16)
            lse_all = jnp.where(lane == hh, m + jnp.log(den), lse_all)
        lse_ref[0, 0, q0:q0 + qb, :] = lse_all


def _band_attention(qkv4):
    batch, dil, sub_len, n = qkv4.shape
    d = n // 3
    lt = min(BAND_LT, sub_len)
    r = DB_RADIUS
    hpt = lt // r
    n_halo = sub_len // r

    def main_spec(which):
        return pl.BlockSpec((1, 1, lt, d), lambda b, rr, i: (b, rr, i, which))

    def left_spec(which):
        return pl.BlockSpec((1, 1, r, d), lambda b, rr, i: (
            b, rr, jnp.maximum(i * hpt - 1, 0), which))

    def right_spec(which):
        return pl.BlockSpec((1, 1, r, d), lambda b, rr, i: (
            b, rr, jnp.minimum((i + 1) * hpt, n_halo - 1), which))

    return pl.pallas_call(
        functools.partial(_band_attn_kernel, lt=lt, sub_len=sub_len),
        grid=(batch, dil, sub_len // lt),
        in_specs=[main_spec(0), main_spec(1), left_spec(1), right_spec(1),
                  main_spec(2), left_spec(2), right_spec(2)],
        out_specs=[pl.BlockSpec((1, 1, lt, d), lambda b, rr, i: (b, rr, i, 0)),
                   pl.BlockSpec((1, 1, lt, LANES), lambda b, rr, i: (b, rr, i, 0))],
        out_shape=[jax.ShapeDtypeStruct((batch, dil, sub_len, d), BF16),
                   jax.ShapeDtypeStruct((batch, dil, sub_len, LANES), F32)],
        scratch_shapes=[pltpu.VMEM((lt + 2 * r, d), BF16),
                        pltpu.VMEM((lt + 2 * r, 2 * d), BF16)],
        compiler_params=_params("arbitrary", "arbitrary", "arbitrary"),
        name=f"band_attention_d{dil}",
    )(qkv4, qkv4, qkv4, qkv4, qkv4, qkv4, qkv4)


def _combine_tail_kernel(o0_ref, o1_ref, o2_ref, l0_ref, l1_ref, l2_ref, w_ref, g_ref, x_ref,
                         *rest, dils):
    tm = x_ref.shape[0]
    n_slab = x_ref.shape[1] // HEAD_W
    o_refs = (o0_ref, o1_ref, o2_ref)
    l_refs = (l0_ref, l1_ref, l2_ref)
    stage = list(rest[N_XA_REFS + N_FF_REFS + 1:])
    o_tok, l_tok = [], []
    for gi, dil in enumerate(dils):
        if dil == 1:
            o_tok.append(None)
            l_tok.append(l_refs[gi][0, 0])
            continue
        so_ref = stage.pop(0)
        sl_ref = stage.pop(0)
        for r in range(dil):
            rows = pl.ds(r, tm // dil, stride=dil)
            sl_ref[rows, :] = l_refs[gi][0, r]
            for hh in range(n_slab):
                so_ref[hh, rows, :] = o_refs[gi][0, r, :, hh * HEAD_W:(hh + 1) * HEAD_W].astype(F32)
        o_tok.append(so_ref)
        l_tok.append(sl_ref[...])
    mx = jnp.maximum(jnp.maximum(l_tok[0], l_tok[1]), l_tok[2])
    es = [jnp.exp(l - mx) for l in l_tok]
    inv = 1.0 / (es[0] + es[1] + es[2])
    ws = [e * inv for e in es]
    outs = []
    for hh in range(n_slab):
        sl = slice(hh * HEAD_W, (hh + 1) * HEAD_W)
        c = slice(hh, hh + 1)
        oh = None
        for gi, dil in enumerate(dils):
            og = o_refs[gi][0, 0, :, sl].astype(F32) if dil == 1 else o_tok[gi][hh]
            term = ws[gi][:, c] * og
            oh = term if oh is None else oh + term
        outs.append(oh.astype(BF16))
    _finish_layer(jnp.concatenate(outs, axis=1), w_ref, g_ref, x_ref, rest)


def _combine_layer_tail(os, lses, w_o, mix_g, x2, xa, ff, seq, mem_len):
    t, d = x2.shape
    tm = min(ROW_TILE, seq)
    nst = seq // tm
    dils = tuple(o.shape[1] for o in os)

    def group_spec(dil, width):
        return pl.BlockSpec((1, dil, tm // dil, width), lambda i: (i // nst, 0, i % nst, 0))

    stage = []
    for dil in dils:
        if dil > 1:
            stage += [pltpu.VMEM((d // HEAD_W, tm, HEAD_W), F32), pltpu.VMEM((tm, LANES), F32)]
    in_specs, row = _tail_specs(tm, d, ff[1].shape[1], mem_len, nst)
    pre_g, wq, kv, wo, post_g = xa
    return pl.pallas_call(
        functools.partial(_combine_tail_kernel, dils=dils),
        grid=(t // tm,),
        in_specs=[group_spec(dil, d) for dil in dils] + [group_spec(dil, LANES) for dil in dils]
                 + in_specs,
        out_specs=row,
        out_shape=jax.ShapeDtypeStruct((t, d), F32),
        scratch_shapes=stage,
        compiler_params=pltpu.CompilerParams(dimension_semantics=("parallel",),
                                             vmem_limit_bytes=TAIL_VMEM_LIMIT),
        name="combine_layer_tail",
    )(*os, *lses, w_o, mix_g, x2, pre_g, wq, kv, kv, wo, post_g, *ff)


def _row(v):
    return v.reshape(1, -1)


def kernel(x, mem, positions, mix_pre_g, mix_post_g, da_w_qkv, da_lambda, da_subln_g, da_w_o,
           db_w_qkv, db_w_o, xa_pre_g, mem_g, xa_wq, xa_wkv, xa_wo, xa_post_g,
           ffn_pre_g, w_up, w_down, ffn_post_g):
    batch, seq, d = x.shape
    mem_len = mem.shape[1]
    depth = mix_pre_g.shape[0]
    n_mixers = 2
    x2 = x.reshape(batch * seq, d)
    mem2 = mem.reshape(batch * mem_len, d)
    pos_col = positions.astype(F32).reshape(batch * seq, 1)

    for i in range(depth):
        j = i // n_mixers
        kv = _mem_kv(mem2, _row(mem_g[i]), xa_wkv[i].astype(BF16))
        xa = (_row(xa_pre_g[i]), xa_wq[i].astype(BF16), kv, xa_wo[i].astype(BF16),
              _row(xa_post_g[i]))
        ff = (_row(ffn_pre_g[i]), w_up[i].astype(BF16), w_down[i].astype(BF16),
              _row(ffn_post_g[i]))
        if i % n_mixers == 0:
            lambda_init = 0.8 - 0.6 * math.exp(-0.3 * i)
            cos, sin = _rope_tables(pos_col, DA_QK_DIM)
            qt, k2, vt = _qkv_diff(x2, _row(mix_pre_g[i]), da_w_qkv[j].astype(BF16), cos, sin,
                                   batch, seq)
            o = _diff_attention(qt, k2.reshape(batch, seq, d), vt, da_lambda[j],
                                _row(da_subln_g[j]), lambda_init)
            x2 = _layer_tail(o.reshape(batch * seq, d), da_w_o[j].astype(BF16),
                             _row(mix_post_g[i]), x2, xa, ff, seq, mem_len)
        else:
            cos, sin = _rope_tables(pos_col, HEAD_W)
            w_qkv = db_w_qkv[j].astype(BF16)
            os, lses = [], []
            for gi, (_, dil) in enumerate(DB_GROUPS):
                qkv4 = _qkv_dilated(x2, _row(mix_pre_g[i]), w_qkv, cos, sin, batch, seq, gi, dil)
                o, lse = _band_attention(qkv4)
                os.append(o)
                lses.append(lse)
            x2 = _combine_layer_tail(os, lses, db_w_o[j].astype(BF16), _row(mix_post_g[i]), x2,
                                     xa, ff, seq, mem_len)
    return x2.reshape(batch, seq, d)
```

```python
import functools
import math

import jax
import jax.numpy as jnp
from jax import lax
from jax.experimental import pallas as pl
from jax.experimental.pallas import tpu as pltpu

F32 = jnp.float32
BF16 = jnp.bfloat16

NORM_EPS = 1e-6
ROPE_THETA = 10000.0
LANES = 128
HEAD_W = 128
DA_HEADS = 8
DA_QK_DIM = 64
DA_VT_ROWS = HEAD_W + 16
DB_GROUPS = ((128, 1), (512, 4), (2048, 16))
DB_HEADS = 8
DB_RADIUS = 64
XA_HEADS = 4
XA_HEAD_DIM = 256
NEG_BIG = -1e30
VMEM_LIMIT = 48 * 1024 * 1024
TAIL_VMEM_LIMIT = 56 * 1024 * 1024
MLP_CHUNK = 1024
TAIL_SPLIT = 2

ROW_TILE = 512
ATT_TQ = 2048
ATT_TK = 512
BAND_LT = 512
BAND_QB = 128


def _params(*sem):
    return pltpu.CompilerParams(dimension_semantics=sem, vmem_limit_bytes=VMEM_LIMIT)


def _dot(a, b):
    return jnp.dot(a, b, preferred_element_type=F32)


def _dot_nt(a, b):
    return lax.dot_general(a, b, (((1,), (1,)), ((), ())), preferred_element_type=F32)


def _rms(x, g):
    ms = jnp.mean(x * x, axis=-1, keepdims=True)
    return x * lax.rsqrt(ms + NORM_EPS) * g


def _rope_slab(t, cos, sin, half):
    if half == 64:
        rot = pltpu.roll(t, 64, 1)
    else:
        lane = lax.broadcasted_iota(jnp.int32, t.shape, 1)
        rot = jnp.where((lane & 32) == 0, pltpu.roll(t, 96, 1), pltpu.roll(t, 32, 1))
    return t * cos + rot * sin


def _rope_table_kernel(pos_ref, inv_ref, sign_ref, cos_ref, sin_ref):
    ang = pos_ref[...] * inv_ref[...]
    cos_ref[...] = jnp.cos(ang)
    sin_ref[...] = jnp.sin(ang) * sign_ref[...]


def _rope_tables(pos_col, dh):
    t = pos_col.shape[0]
    inv = ROPE_THETA ** (-jnp.arange(0, dh, 2, dtype=F32) / dh)
    reps = LANES // dh
    inv_full = jnp.tile(jnp.concatenate([inv, inv]), reps).reshape(1, LANES)
    sign = jnp.tile(jnp.concatenate([-jnp.ones(dh // 2, F32), jnp.ones(dh // 2, F32)]),
                    reps).reshape(1, LANES)
    tm = min(t, 2048)
    return pl.pallas_call(
        _rope_table_kernel,
        grid=(t // tm,),
        in_specs=[pl.BlockSpec((tm, 1), lambda i: (i, 0)),
                  pl.BlockSpec((1, LANES), lambda i: (0, 0)),
                  pl.BlockSpec((1, LANES), lambda i: (0, 0))],
        out_specs=[pl.BlockSpec((tm, LANES), lambda i: (i, 0)),
                   pl.BlockSpec((tm, LANES), lambda i: (i, 0))],
        out_shape=[jax.ShapeDtypeStruct((t, LANES), F32)] * 2,
        compiler_params=_params("parallel"),
        name="rope_tables",
    )(pos_col, inv_full, sign)


def _qkv_diff_kernel(x_ref, g_ref, w_ref, cos_ref, sin_ref, qt_ref, k_ref, vt_ref, *, scale):
    d = x_ref.shape[1]
    h = _rms(x_ref[...], g_ref[...]).astype(BF16)
    cos = cos_ref[...]
    sin = sin_ref[...]
    q = _dot(h, w_ref[:, 0:d])
    for hh in range(DA_HEADS):
        sl = slice(hh * HEAD_W, (hh + 1) * HEAD_W)
        t = _rope_slab(q[:, sl], cos, sin, DA_QK_DIM // 2) * scale
        qt_ref[0, 0, sl, :] = t.T.astype(BF16)
    k = _dot(h, w_ref[:, d:2 * d])
    for hh in range(DA_HEADS):
        sl = slice(hh * HEAD_W, (hh + 1) * HEAD_W)
        k_ref[:, sl] = _rope_slab(k[:, sl], cos, sin, DA_QK_DIM // 2).astype(BF16)
    v = _dot(h, w_ref[:, 2 * d:3 * d])
    ones = jnp.ones((DA_VT_ROWS - HEAD_W, x_ref.shape[0]), BF16)
    for hh in range(DA_HEADS):
        sl = slice(hh * HEAD_W, (hh + 1) * HEAD_W)
        r0 = hh * DA_VT_ROWS
        vt_ref[0, 0, r0:r0 + HEAD_W, :] = v[:, sl].T.astype(BF16)
        vt_ref[0, 0, r0 + HEAD_W:r0 + DA_VT_ROWS, :] = ones


def _qkv_diff(x2, g, w_bf, cos, sin, batch, seq):
    t, d = x2.shape
    tm = min(ATT_TK, seq)
    nst = seq // tm
    scale = DA_QK_DIM ** -0.5 * math.log2(math.e)
    vt_rows = DA_HEADS * DA_VT_ROWS
    return pl.pallas_call(
        functools.partial(_qkv_diff_kernel, scale=scale),
        grid=(t // tm,),
        in_specs=[pl.BlockSpec((tm, d), lambda i: (i, 0)),
                  pl.BlockSpec((1, d), lambda i: (0, 0)),
                  pl.BlockSpec((d, 3 * d), lambda i: (0, 0)),
                  pl.BlockSpec((tm, LANES), lambda i: (i, 0)),
                  pl.BlockSpec((tm, LANES), lambda i: (i, 0))],
        out_specs=[pl.BlockSpec((1, 1, d, tm), lambda i: (i // nst, i % nst, 0, 0)),
                   pl.BlockSpec((tm, d), lambda i: (i, 0)),
                   pl.BlockSpec((1, 1, vt_rows, tm), lambda i: (i // nst, i % nst, 0, 0))],
        out_shape=[jax.ShapeDtypeStruct((batch, nst, d, tm), BF16),
                   jax.ShapeDtypeStruct((t, d), BF16),
                   jax.ShapeDtypeStruct((batch, nst, vt_rows, tm), BF16)],
        compiler_params=_params("parallel"),
        name="qkv_diff",
    )(x2, g, w_bf, cos, sin)


def _diff_attn_kernel(qt_ref, k_ref, vt_ref, lam_ref, g_ref, o_ref,
                      qp_ref, s0_ref, s1_ref, e0_ref, e1_ref, t0_ref, t1_ref, a0_ref, a1_ref,
                      m_ref, acc_ref, *, nk, tk, lambda_init):
    qt = jnp.concatenate([qt_ref[0, j] for j in range(qt_ref.shape[1])], axis=1)
    row = lax.broadcasted_iota(jnp.int32, qt.shape, 0)
    zero = jnp.zeros_like(qt)
    qp_ref[0] = jnp.where(row < DA_QK_DIM, qt, zero)
    qp_ref[1] = jnp.where(row >= DA_QK_DIM, qt, zero)
    m_ref[...] = jnp.full(m_ref.shape, NEG_BIG, F32)
    acc_ref[...] = jnp.zeros(acc_ref.shape, F32)

    def scores(t, s_ref, tmax_ref):
        kb = k_ref[0, pl.ds(pl.multiple_of(t * tk, tk), tk), :]
        for c in range(2):
            s = _dot(kb, qp_ref[c])
            s_ref[c] = s
            tmax_ref[c] = jnp.max(s, axis=0, keepdims=True)

    def exponentiate(s_ref, tmax_ref, e_ref, alpha_ref):
        for c in range(2):
            m_prev = m_ref[c]
            m_new = jnp.maximum(m_prev, tmax_ref[c])
            alpha_ref[c] = jnp.exp2(m_prev - m_new)
            m_ref[c] = m_new
            e_ref[c] = jnp.exp2(s_ref[c] - m_new).astype(BF16)

    def accumulate(t, e_ref, alpha_ref):
        vt = vt_ref[0, t]
        for c in range(2):
            acc_ref[c] = alpha_ref[c] * acc_ref[c] + _dot(vt, e_ref[c])

    def half_step(t, s_next, tmax_next, s_cur, tmax_cur, e_cur, a_cur, e_prev, a_prev,
                  with_scores=True, with_acc=True):
        if with_scores:
            scores(t + 1, s_next, tmax_next)
        exponentiate(s_cur, tmax_cur, e_cur, a_cur)
        if with_acc:
            accumulate(t - 1, e_prev, a_prev)

    even = (s1_ref, t1_ref, s0_ref, t0_ref, e0_ref, a0_ref, e1_ref, a1_ref)
    odd = (s0_ref, t0_ref, s1_ref, t1_ref, e1_ref, a1_ref, e0_ref, a0_ref)

    scores(0, s0_ref, t0_ref)
    half_step(0, *even, with_acc=False)
    half_step(1, *odd)

    def kv_pair(u, carry):
        t = 2 * u
        half_step(t, *even)
        half_step(t + 1, *odd)
        return carry

    lax.fori_loop(1, nk // 2 - 1, kv_pair, 0)
    half_step(nk - 2, *even)
    half_step(nk - 1, *odd, with_scores=False)
    accumulate(nk - 1, e1_ref, a1_ref)

    lam = lam_ref[...]
    lam_full = (jnp.exp(jnp.sum(lam[0:1] * lam[1:2], axis=-1, keepdims=True))
                - jnp.exp(jnp.sum(lam[2:3] * lam[3:4], axis=-1, keepdims=True))
                + lambda_init)
    inv1 = 1.0 / acc_ref[0, HEAD_W:HEAD_W + 1, :]
    inv2 = 1.0 / acc_ref[1, HEAD_W:HEAD_W + 1, :]
    ot = acc_ref[0, 0:HEAD_W, :] * inv1 - lam_full * (acc_ref[1, 0:HEAD_W, :] * inv2)
    o = _rms(ot.T, g_ref[...]) * (1.0 - lambda_init)
    o_ref[0] = o.astype(BF16)


def _diff_attention(qt, k3, vt, lam, subln_g, lambda_init):
    batch, nk, d, tk = qt.shape
    seq = k3.shape[1]
    tq = min(ATT_TQ, seq)
    nqt = tq // tk
    nq = seq // tq
    assert nk % 2 == 0 and nk >= 4 and tq % tk == 0
    return pl.pallas_call(
        functools.partial(_diff_attn_kernel, nk=nk, tk=tk, lambda_init=lambda_init),
        grid=(batch, DA_HEADS, nq),
        in_specs=[pl.BlockSpec((1, nqt, HEAD_W, tk), lambda b, h, i: (b, i, h, 0)),
                  pl.BlockSpec((1, seq, HEAD_W), lambda b, h, i: (b, 0, h)),
                  pl.BlockSpec((1, nk, DA_VT_ROWS, tk), lambda b, h, i: (b, 0, h, 0)),
                  pl.BlockSpec(lam.shape, lambda b, h, i: (0, 0)),
                  pl.BlockSpec((1, HEAD_W), lambda b, h, i: (0, 0))],
        out_specs=pl.BlockSpec((1, tq, HEAD_W), lambda b, h, i: (b, i, h)),
        out_shape=jax.ShapeDtypeStruct((batch, seq, d), BF16),
        scratch_shapes=[pltpu.VMEM((2, HEAD_W, tq), BF16),
                        pltpu.VMEM((2, tk, tq), F32),
                        pltpu.VMEM((2, tk, tq), F32),
                        pltpu.VMEM((2, tk, tq), BF16),
                        pltpu.VMEM((2, tk, tq), BF16),
                        pltpu.VMEM((2, 1, tq), F32),
                        pltpu.VMEM((2, 1, tq), F32),
                        pltpu.VMEM((2, 1, tq), F32),
                        pltpu.VMEM((2, 1, tq), F32),
                        pltpu.VMEM((2, 1, tq), F32),
                        pltpu.VMEM((2, DA_VT_ROWS, tq), F32)],
        compiler_params=_params("parallel", "parallel", "parallel"),
        name="diff_attention",
    )(qt, k3, vt, lam, subln_g)


def _mem_kv_kernel(mem_ref, g_ref, w_ref, o_ref):
    mn = _rms(mem_ref[...], g_ref[...]).astype(BF16)
    o_ref[...] = _dot(mn, w_ref[...]).astype(BF16)


def _mem_kv(mem2, g, w_bf):
    rows, d = mem2.shape
    n = w_bf.shape[1]
    tn = n // 2
    return pl.pallas_call(
        _mem_kv_kernel,
        grid=(n // tn,),
        in_specs=[pl.BlockSpec((rows, d), lambda j: (0, 0)),
                  pl.BlockSpec((1, d), lambda j: (0, 0)),
                  pl.BlockSpec((d, tn), lambda j: (0, j))],
        out_specs=pl.BlockSpec((rows, tn), lambda j: (0, j)),
        out_shape=jax.ShapeDtypeStruct((rows, n), BF16),
        compiler_params=_params("parallel"),
        name="mem_kv",
    )(mem2, g, w_bf)


def _cross_attn_heads(q, k_ref, v_ref):
    outs = []
    for hh in range(XA_HEADS):
        sl = slice(hh * XA_HEAD_DIM, (hh + 1) * XA_HEAD_DIM)
        s = _dot_nt(q[:, sl], k_ref[:, sl])
        e = jnp.exp(s - jnp.max(s, axis=-1, keepdims=True))
        p = e * (1.0 / jnp.sum(e, axis=-1, keepdims=True))
        outs.append(_dot(p.astype(BF16), v_ref[:, sl]).astype(BF16))
    return jnp.concatenate(outs, axis=1)


N_XA_REFS = 6
N_FF_REFS = 4


def _finish_layer(a, w_ref, g_ref, x_ref, rest):
    pre_g_ref, wq_ref, k_ref, v_ref, wo_ref, post_g_ref = rest[:N_XA_REFS]
    fpre_ref, wu_ref, wd_ref, fpost_ref = rest[N_XA_REFS:N_XA_REFS + N_FF_REFS]
    out_ref = rest[N_XA_REFS + N_FF_REFS]
    hs = x_ref.shape[0] // TAIL_SPLIT
    rows = [slice(p * hs, (p + 1) * hs) for p in range(TAIL_SPLIT)]
    parts = range(TAIL_SPLIT)
    y = [_dot(a[rows[p]], w_ref[...]) for p in parts]
    x = [x_ref[rows[p], :] + _rms(y[p], g_ref[...]) for p in parts]
    h = [_rms(x[p], pre_g_ref[...]).astype(BF16) for p in parts]
    q = [(_dot(h[p], wq_ref[...]) * (XA_HEAD_DIM ** -0.5)).astype(BF16) for p in parts]
    o = [_cross_attn_heads(q[p], k_ref, v_ref) for p in parts]
    y = [_dot(o[p], wo_ref[...]) for p in parts]
    x = [x[p] + _rms(y[p], post_g_ref[...]) for p in parts]
    h = [_rms(x[p], fpre_ref[...]).astype(BF16) for p in parts]
    d_ff = wu_ref.shape[1]
    chunk = min(MLP_CHUNK, d_ff)
    acc = [jnp.zeros(x[p].shape, F32) for p in parts]
    for c in range(d_ff // chunk):
        sl = slice(c * chunk, (c + 1) * chunk)
        for p in parts:
            u = jnp.maximum(_dot(h[p], wu_ref[:, sl]), 0.0)
            acc[p] = acc[p] + _dot((u * u).astype(BF16), wd_ref[sl, :])
    for p in parts:
        out_ref[rows[p], :] = x[p] + _rms(acc[p], fpost_ref[...])


def _tail_kernel(a_ref, w_ref, g_ref, x_ref, *rest):
    _finish_layer(a_ref[...], w_ref, g_ref, x_ref, rest)


def _tail_specs(tm, d, d_ff, mem_len, nst):
    def const(*shape):
        return pl.BlockSpec(shape, lambda i: (0,) * len(shape), pipeline_mode=pl.Buffered(1))

    row = pl.BlockSpec((tm, d), lambda i: (i, 0))
    in_specs = [const(d, d), const(1, d), row,
                const(1, d), const(d, d),
                pl.BlockSpec((mem_len, d), lambda i: (i // nst, 0)),
                pl.BlockSpec((mem_len, d), lambda i: (i // nst, 1)),
                const(d, d), const(1, d),
                const(1, d), const(d, d_ff), const(d_ff, d), const(1, d)]
    return in_specs, row


def _layer_tail(a2, w_o, mix_g, x2, xa, ff, seq, mem_len):
    t, d = x2.shape
    tm = min(ROW_TILE, seq)
    in_specs, row = _tail_specs(tm, d, ff[1].shape[1], mem_len, seq // tm)
    pre_g, wq, kv, wo, post_g = xa
    return pl.pallas_call(
        _tail_kernel,
        grid=(t // tm,),
        in_specs=[row] + in_specs,
        out_specs=row,
        out_shape=jax.ShapeDtypeStruct((t, d), F32),
        compiler_params=pltpu.CompilerParams(dimension_semantics=("parallel",),
                                             vmem_limit_bytes=TAIL_VMEM_LIMIT),
        name="layer_tail",
    )(a2, w_o, mix_g, x2, pre_g, wq, kv, kv, wo, post_g, *ff)


def _qkv_dil_kernel(x_ref, g_ref, wq_ref, wk_ref, wv_ref, cos_ref, sin_ref, o_ref, *stage,
                    scale, dil):
    tm, d = x_ref.shape
    lt = tm // dil
    n_slab = d // HEAD_W
    hn = _rms(x_ref[...], g_ref[...])
    if dil == 1:
        h = hn.astype(BF16)
        cos = cos_ref[...]
        sin = sin_ref[...]
    else:
        f_ref, hp_ref, tp_ref = stage
        for hh in range(n_slab):
            f_ref[hh] = hn[:, hh * HEAD_W:(hh + 1) * HEAD_W]
        f_ref[n_slab] = cos_ref[...]
        f_ref[n_slab + 1] = sin_ref[...]
        for r in range(dil):
            src = pl.ds(r, lt, stride=dil)
            dst = slice(r * lt, (r + 1) * lt)
            for hh in range(n_slab):
                hp_ref[dst, hh * HEAD_W:(hh + 1) * HEAD_W] = f_ref[hh, src, :].astype(BF16)
            tp_ref[0, dst, :] = f_ref[n_slab, src, :]
            tp_ref[1, dst, :] = f_ref[n_slab + 1, src, :]
        h = hp_ref[...]
        cos = tp_ref[0]
        sin = tp_ref[1]
    for which in range(3):
        y = _dot(h, (wq_ref, wk_ref, wv_ref)[which][...])
        for hh in range(n_slab):
            col = slice(which * d + hh * HEAD_W, which * d + (hh + 1) * HEAD_W)
            t = y[:, hh * HEAD_W:(hh + 1) * HEAD_W]
            if which < 2:
                t = _rope_slab(t, cos, sin, HEAD_W // 2)
            if which == 0:
                t = t * scale
            t = t.astype(BF16)
            for r in range(dil):
                o_ref[0, r, :, col] = t[r * lt:(r + 1) * lt]


def _qkv_dilated(x2, g, w_bf, cos, sin, batch, seq, group, dil):
    groups = len(DB_GROUPS)
    t, d = x2.shape
    tm = min(ROW_TILE, seq)
    nst = seq // tm
    lt = tm // dil
    stage = []
    if dil > 1:
        stage = [pltpu.VMEM((d // HEAD_W + 2, tm, HEAD_W), F32),
                 pltpu.VMEM((tm, d), BF16),
                 pltpu.VMEM((2, tm, HEAD_W), F32)]
    return pl.pallas_call(
        functools.partial(_qkv_dil_kernel, scale=HEAD_W ** -0.5 * math.log2(math.e), dil=dil),
        grid=(t // tm,),
        in_specs=[pl.BlockSpec((tm, d), lambda i: (i, 0)),
                  pl.BlockSpec((1, d), lambda i: (0, 0)),
                  pl.BlockSpec((d, d), lambda i: (0, group)),
                  pl.BlockSpec((d, d), lambda i: (0, groups + group)),
                  pl.BlockSpec((d, d), lambda i: (0, 2 * groups + group)),
                  pl.BlockSpec((tm, LANES), lambda i: (i, 0)),
                  pl.BlockSpec((tm, LANES), lambda i: (i, 0))],
        out_specs=pl.BlockSpec((1, dil, lt, 3 * d), lambda i: (i // nst, 0, i % nst, 0)),
        out_shape=jax.ShapeDtypeStruct((batch, dil, seq // dil, 3 * d), BF16),
        scratch_shapes=stage,
        compiler_params=_params("parallel"),
        name=f"qkv_dilated_d{dil}",
    )(x2, g, w_bf, w_bf, w_bf, cos, sin)


def _band_attn_kernel(q_ref, km_ref, kl_ref, kr_ref, vm_ref, vl_ref, vr_ref, o_ref, lse_ref,
                      kc_ref, vc_ref, *, lt, sub_len):
    i = pl.program_id(2)
    r = DB_RADIUS

    @pl.when((pl.program_id(0) == 0) & (pl.program_id(1) == 0) & (i == 0))
    def _():
        vc_ref[...] = jnp.ones(vc_ref.shape, BF16)

    kc_ref[0:r] = kl_ref[0, 0]
    kc_ref[r:r + lt] = km_ref[0, 0]
    kc_ref[r + lt:r + lt + r] = kr_ref[0, 0]
    for hh in range(DB_HEADS):
        sl = slice(hh * HEAD_W, (hh + 1) * HEAD_W)
        dst = slice(2 * hh * HEAD_W, (2 * hh + 1) * HEAD_W)
        vc_ref[0:r, dst] = vl_ref[0, 0, :, sl]
        vc_ref[r:r + lt, dst] = vm_ref[0, 0, :, sl]
        vc_ref[r + lt:r + lt + r, dst] = vr_ref[0, 0, :, sl]

    qb = BAND_QB
    kb = qb + 2 * r
    qi = lax.broadcasted_iota(jnp.int32, (qb, kb), 0)
    ki = lax.broadcasted_iota(jnp.int32, (qb, kb), 1)
    band = jnp.abs(ki - r - qi) <= r
    lane = lax.broadcasted_iota(jnp.int32, (qb, LANES), 1)

    for j in range(lt // qb):
        q0 = j * qb
        key_pos = i * lt + (q0 - r) + ki
        mask = band & (key_pos >= 0) & (key_pos < sub_len)
        lse_all = jnp.zeros((qb, LANES), F32)
        for hh in range(DB_HEADS):
            sl = slice(hh * HEAD_W, (hh + 1) * HEAD_W)
            s = _dot_nt(q_ref[0, 0, q0:q0 + qb, sl], kc_ref[q0:q0 + kb, sl])
            s = jnp.where(mask, s, NEG_BIG)
            m = jnp.max(s, axis=-1, keepdims=True)
            p = jnp.exp2(s - m).astype(BF16)
            oa = _dot(p, vc_ref[q0:q0 + kb, 2 * hh * HEAD_W:(2 * hh + 2) * HEAD_W])
            den = oa[:, HEAD_W:]
            o_ref[0, 0, q0:q0 + qb, sl] = (oa[:, :HEAD_W] * (1.0 / den)).astype(BF16)
            lse_all = jnp.where(lane == hh, m + jnp.log2(den), lse_all)
        lse_ref[0, 0, q0:q0 + qb, :] = lse_all


def _band_attention(qkv4):
    batch, dil, sub_len, n = qkv4.shape
    d = n // 3
    lt = min(BAND_LT, sub_len)
    r = DB_RADIUS
    hpt = lt // r
    n_halo = sub_len // r

    def main_spec(which):
        return pl.BlockSpec((1, 1, lt, d), lambda b, rr, i: (b, rr, i, which))

    def left_spec(which):
        return pl.BlockSpec((1, 1, r, d), lambda b, rr, i: (
            b, rr, jnp.maximum(i * hpt - 1, 0), which))

    def right_spec(which):
        return pl.BlockSpec((1, 1, r, d), lambda b, rr, i: (
            b, rr, jnp.minimum((i + 1) * hpt, n_halo - 1), which))

    return pl.pallas_call(
        functools.partial(_band_attn_kernel, lt=lt, sub_len=sub_len),
        grid=(batch, dil, sub_len // lt),
        in_specs=[main_spec(0), main_spec(1), left_spec(1), right_spec(1),
                  main_spec(2), left_spec(2), right_spec(2)],
        out_specs=[pl.BlockSpec((1, 1, lt, d), lambda b, rr, i: (b, rr, i, 0)),
                   pl.BlockSpec((1, 1, lt, LANES), lambda b, rr, i: (b, rr, i, 0))],
        out_shape=[jax.ShapeDtypeStruct((batch, dil, sub_len, d), BF16),
                   jax.ShapeDtypeStruct((batch, dil, sub_len, LANES), F32)],
        scratch_shapes=[pltpu.VMEM((lt + 2 * r, d), BF16),
                        pltpu.VMEM((lt + 2 * r, 2 * d), BF16)],
        compiler_params=_params("arbitrary", "arbitrary", "arbitrary"),
        name=f"band_attention_d{dil}",
    )(qkv4, qkv4, qkv4, qkv4, qkv4, qkv4, qkv4)


def _combine_tail_kernel(o0_ref, o1_ref, o2_ref, l0_ref, l1_ref, l2_ref, w_ref, g_ref, x_ref,
                         *rest, dils):
    tm = x_ref.shape[0]
    n_slab = x_ref.shape[1] // HEAD_W
    o_refs = (o0_ref, o1_ref, o2_ref)
    l_refs = (l0_ref, l1_ref, l2_ref)
    stage = list(rest[N_XA_REFS + N_FF_REFS + 1:])
    o_tok, l_tok = [], []
    for gi, dil in enumerate(dils):
        if dil == 1:
            o_tok.append(None)
            l_tok.append(l_refs[gi][0, 0])
            continue
        so_ref = stage.pop(0)
        sl_ref = stage.pop(0)
        for r in range(dil):
            rows = pl.ds(r, tm // dil, stride=dil)
            sl_ref[rows, :] = l_refs[gi][0, r]
            for hh in range(n_slab):
                so_ref[hh, rows, :] = o_refs[gi][0, r, :, hh * HEAD_W:(hh + 1) * HEAD_W].astype(F32)
        o_tok.append(so_ref)
        l_tok.append(sl_ref[...])
    mx = jnp.maximum(jnp.maximum(l_tok[0], l_tok[1]), l_tok[2])
    es = [jnp.exp2(l - mx) for l in l_tok]
    inv = 1.0 / (es[0] + es[1] + es[2])
    ws = [e * inv for e in es]
    outs = []
    for hh in range(n_slab):
        sl = slice(hh * HEAD_W, (hh + 1) * HEAD_W)
        c = slice(hh, hh + 1)
        oh = None
        for gi, dil in enumerate(dils):
            og = o_refs[gi][0, 0, :, sl].astype(F32) if dil == 1 else o_tok[gi][hh]
            term = ws[gi][:, c] * og
            oh = term if oh is None else oh + term
        outs.append(oh.astype(BF16))
    _finish_layer(jnp.concatenate(outs, axis=1), w_ref, g_ref, x_ref, rest)


def _combine_layer_tail(os, lses, w_o, mix_g, x2, xa, ff, seq, mem_len):
    t, d = x2.shape
    tm = min(ROW_TILE, seq)
    nst = seq // tm
    dils = tuple(o.shape[1] for o in os)

    def group_spec(dil, width):
        return pl.BlockSpec((1, dil, tm // dil, width), lambda i: (i // nst, 0, i % nst, 0))

    stage = []
    for dil in dils:
        if dil > 1:
            stage += [pltpu.VMEM((d // HEAD_W, tm, HEAD_W), F32), pltpu.VMEM((tm, LANES), F32)]
    in_specs, row = _tail_specs(tm, d, ff[1].shape[1], mem_len, nst)
    pre_g, wq, kv, wo, post_g = xa
    return pl.pallas_call(
        functools.partial(_combine_tail_kernel, dils=dils),
        grid=(t // tm,),
        in_specs=[group_spec(dil, d) for dil in dils] + [group_spec(dil, LANES) for dil in dils]
                 + in_specs,
        out_specs=row,
        out_shape=jax.ShapeDtypeStruct((t, d), F32),
        scratch_shapes=stage,
        compiler_params=pltpu.CompilerParams(dimension_semantics=("parallel",),
                                             vmem_limit_bytes=TAIL_VMEM_LIMIT),
        name="combine_layer_tail",
    )(*os, *lses, w_o, mix_g, x2, pre_g, wq, kv, kv, wo, post_g, *ff)


def _row(v):
    return v.reshape(1, -1)


def kernel(x, mem, positions, mix_pre_g, mix_post_g, da_w_qkv, da_lambda, da_subln_g, da_w_o,
           db_w_qkv, db_w_o, xa_pre_g, mem_g, xa_wq, xa_wkv, xa_wo, xa_post_g,
           ffn_pre_g, w_up, w_down, ffn_post_g):
    batch, seq, d = x.shape
    mem_len = mem.shape[1]
    depth = mix_pre_g.shape[0]
    n_mixers = 2
    x2 = x.reshape(batch * seq, d)
    mem2 = mem.reshape(batch * mem_len, d)
    pos_col = positions.astype(F32).reshape(batch * seq, 1)

    for i in range(depth):
        j = i // n_mixers
        kv = _mem_kv(mem2, _row(mem_g[i]), xa_wkv[i].astype(BF16))
        xa = (_row(xa_pre_g[i]), xa_wq[i].astype(BF16), kv, xa_wo[i].astype(BF16),
              _row(xa_post_g[i]))
        ff = (_row(ffn_pre_g[i]), w_up[i].astype(BF16), w_down[i].astype(BF16),
              _row(ffn_post_g[i]))
        if i % n_mixers == 0:
            lambda_init = 0.8 - 0.6 * math.exp(-0.3 * i)
            cos, sin = _rope_tables(pos_col, DA_QK_DIM)
            qt, k2, vt = _qkv_diff(x2, _row(mix_pre_g[i]), da_w_qkv[j].astype(BF16), cos, sin,
                                   batch, seq)
            o = _diff_attention(qt, k2.reshape(batch, seq, d), vt, da_lambda[j],
                                _row(da_subln_g[j]), lambda_init)
            x2 = _layer_tail(o.reshape(batch * seq, d), da_w_o[j].astype(BF16),
                             _row(mix_post_g[i]), x2, xa, ff, seq, mem_len)
        else:
            cos, sin = _rope_tables(pos_col, HEAD_W)
            w_qkv = db_w_qkv[j].astype(BF16)
            os, lses = [], []
            for gi, (_, dil) in enumerate(DB_GROUPS):
                qkv4 = _qkv_dilated(x2, _row(mix_pre_g[i]), w_qkv, cos, sin, batch, seq, gi, dil)
                o, lse = _band_attention(qkv4)
                os.append(o)
                lses.append(lse)
            x2 = _combine_layer_tail(os, lses, db_w_o[j].astype(BF16), _row(mix_post_g[i]), x2,
                                     xa, ff, seq, mem_len)
    return x2.reshape(batch, seq, d)
```
